```python
import math
import jax, jax.numpy as jnp
from jax import lax
import numpy as np

D_MODEL = 1024
BATCH = 16
SEQ = 4096
DEPTH = 4

N_MIXERS = 3
EPS = 1e-6
NEG_BIG = -1e30

POOL_WINDOWS = (2, 4, 8, 16)
POOL_WIDTH = 2 * D_MODEL
POOL_GROUP = POOL_WIDTH // len(POOL_WINDOWS)

ATT_GROUPS = ((128, 1), (512, 4), (2048, 16))
ATT_HEAD_DIM = 128
ATT_HEADS = D_MODEL // ATT_HEAD_DIM
ATT_WIDTH = ATT_HEADS * ATT_HEAD_DIM
ROPE_DIM = ATT_HEAD_DIM // 4
ROPE_THETA = 500000.0

RET_HEADS = 4
RET_QK_DIM = D_MODEL // RET_HEADS
RET_V_DIM = 2 * D_MODEL // RET_HEADS
RET_QK = RET_HEADS * RET_QK_DIM
RET_V = RET_HEADS * RET_V_DIM
RET_CHUNK = 128
RET_THETA = 10000.0
RET_DECAY_BASE = 5.0

N_POOL_LAYERS = len(range(0, DEPTH, N_MIXERS))
N_ATT_LAYERS = len(range(1, DEPTH, N_MIXERS))
N_RET_LAYERS = len(range(2, DEPTH, N_MIXERS))

kernel_name = "hybrid_pool_dilattn_retention_encoder"

F32 = jnp.float32


def rms_norm(x, w):
    x32 = x.astype(F32)
    y = x32 * lax.rsqrt(jnp.mean(x32 * x32, axis=-1, keepdims=True) + EPS)
    return (y * w.astype(F32)).astype(x.dtype)


def rotary(x, pos, rot_dim, theta):
    half = rot_dim // 2
    inv_freq = 1.0 / (theta ** (jnp.arange(half, dtype=F32) * 2.0 / rot_dim))
    ang = pos.astype(F32)[:, None] * inv_freq[None, :]
    cos = jnp.cos(ang)[None, :, None, :]
    sin = jnp.sin(ang)[None, :, None, :]
    xr = x[..., :rot_dim].astype(F32)
    x1, x2 = xr[..., :half], xr[..., half:]
    rot = jnp.concatenate([x1 * cos - x2 * sin, x2 * cos + x1 * sin], axis=-1).astype(x.dtype)
    return jnp.concatenate([rot, x[..., rot_dim:]], axis=-1)


def centred_window_mean(u, window):
    B, S, C = u.shape
    cs = jnp.concatenate([jnp.zeros((B, 1, C), F32), jnp.cumsum(u.astype(F32), axis=1)], axis=1)
    t = jnp.arange(S)
    lo = jnp.clip(t - window // 2, 0, S)
    hi = jnp.clip(t - window // 2 + window, 0, S)
    count = (hi - lo).astype(F32)
    return (cs[:, hi] - cs[:, lo]) / count[None, :, None]


def pool_mixer(h, w_in, w_group, scale, w_out):
    B, S, _ = h.shape
    proj = h @ w_in
    u, gate = proj[..., :POOL_WIDTH], proj[..., POOL_WIDTH:]
    ug = u.reshape(B, S, len(POOL_WINDOWS), POOL_GROUP)
    pooled = jnp.stack([centred_window_mean(ug[:, :, g], w) for g, w in enumerate(POOL_WINDOWS)], axis=2)
    diff = (pooled - ug.astype(F32)).astype(h.dtype)
    mixed = jnp.einsum('bsgc,gcd->bsgd', diff, w_group).reshape(B, S, POOL_WIDTH) * scale
    return (mixed * jax.nn.silu(gate)) @ w_out


def dilated_window_attention(q, k, v, dilation, n_side):
    B, S, H, Dh = q.shape
    blk = n_side
    seg = dilation * blk
    Sp = -(-S // seg) * seg
    nblk = Sp // seg

    def to_sub(t):
        t = jnp.pad(t, ((0, 0), (0, Sp - S), (0, 0), (0, 0)))
        return t.reshape(B, nblk, blk, dilation, H, Dh).transpose(0, 3, 1, 2, 4, 5)

    def band(t):
        tp = jnp.pad(t, ((0, 0), (0, 0), (1, 1), (0, 0), (0, 0), (0, 0)))
        return jnp.concatenate([tp[:, :, :-2], tp[:, :, 1:-1], tp[:, :, 2:]], axis=3)

    qb = to_sub(q)
    kw, vw = band(to_sub(k)), band(to_sub(v))

    r = jnp.arange(dilation)[:, None, None]
    bidx = jnp.arange(nblk)[None, :, None]
    c = jnp.arange(3 * blk)
    key_sub = (bidx - 1) * blk + c[None, None, :]
    key_pos = key_sub * dilation + r
    key_ok = (key_sub >= 0) & (key_pos < S)
    offs = c[None, :] - blk - jnp.arange(blk)[:, None]
    band_ok = jnp.abs(offs) <= n_side
    mask = key_ok[:, :, None, :] & band_ok[None, None]

    scores = jnp.einsum('brnqhe,brnkhe->brnhqk', qb, kw).astype(F32) * (Dh ** -0.5)
    scores = jnp.where(mask[None, :, :, None], scores, NEG_BIG)
    m = jnp.max(scores, axis=-1, keepdims=True)
    p = jnp.exp(scores - m)
    den = jnp.sum(p, axis=-1, keepdims=True)
    o = jnp.einsum('brnhqk,brnkhe->brnhqe', p, vw.astype(F32)) / den
    lse = (m + jnp.log(den))[..., 0]
    o = o.transpose(0, 2, 4, 1, 3, 5).reshape(B, Sp, H, Dh)[:, :S]
    lse = lse.transpose(0, 2, 4, 1, 3).reshape(B, Sp, H)[:, :S]
    return o, lse


def dilated_attention_mixer(h, w_in, q_norm, k_norm, w_out):
    B, S, _ = h.shape
    n_g = len(ATT_GROUPS)
    proj = h @ w_in
    qkv = proj[..., :3 * n_g * ATT_WIDTH].reshape(B, S, n_g, 3, ATT_HEADS, ATT_HEAD_DIM)
    gate = proj[..., 3 * n_g * ATT_WIDTH:]
    pos = jnp.arange(S)
    outs, lses = [], []
    for g, (window, dil) in enumerate(ATT_GROUPS):
        q = rotary(rms_norm(qkv[:, :, g, 0], q_norm[g]), pos, ROPE_DIM, ROPE_THETA)
        k = rotary(rms_norm(qkv[:, :, g, 1], k_norm[g]), pos, ROPE_DIM, ROPE_THETA)
        o, lse = dilated_window_attention(q, k, qkv[:, :, g, 2], dil, window // (2 * dil))
        outs.append(o)
        lses.append(lse)
    weights = jax.nn.softmax(jnp.stack(lses, axis=0), axis=0)
    o = jnp.einsum('gbsh,gbshe->bshe', weights, jnp.stack(outs, axis=0))
    y = o.reshape(B, S, ATT_WIDTH).astype(h.dtype) * jax.nn.silu(gate)
    return y @ w_out


def retention_scan(q, k, v, log_gamma, include_diag):
    B, S, H, dk = q.shape
    dv = v.shape[-1]
    C = RET_CHUNK
    N = S // C

    def chunks(t):
        return t.astype(F32).reshape(B, N, C, H, t.shape[-1]).transpose(1, 0, 3, 2, 4)

    qc, kc, vc = chunks(q), chunks(k), chunks(v)
    i = jnp.arange(C, dtype=F32)
    diff = i[:, None] - i[None, :]
    keep = (diff >= 0) if include_diag else (diff > 0)
    inner_decay = jnp.where(keep[None], jnp.exp(jnp.where(keep, diff, 0.0)[None] * log_gamma[:, None, None]), 0.0)
    q_decay = jnp.exp((i + 1.0)[None, :] * log_gamma[:, None])[..., None]
    k_decay = jnp.exp((C - 1.0 - i)[None, :] * log_gamma[:, None])[..., None]
    chunk_decay = jnp.exp(C * log_gamma)[:, None, None]

    def step(state, inp):
        qi, ki, vi = inp
        inner = jnp.einsum('bhqd,bhkd->bhqk', qi, ki) * inner_decay
        out = jnp.einsum('bhqk,bhkv->bhqv', inner, vi) + jnp.einsum('bhqd,bhdv->bhqv', qi * q_decay, state)
        state = state * chunk_decay + jnp.einsum('bhkd,bhkv->bhdv', ki * k_decay, vi)
        return state, out

    _, out = lax.scan(step, jnp.zeros((B, H, dk, dv), F32), (qc, kc, vc))
    return out.transpose(1, 0, 3, 2, 4).reshape(B, S, H, dv)


def retention_mixer(h, w_in, decay_exp, w_out):
    B, S, _ = h.shape
    proj = h @ w_in
    qf, kf, qb, kb, v, gate = jnp.split(proj, [RET_QK, 2 * RET_QK, 3 * RET_QK, 4 * RET_QK, 4 * RET_QK + RET_V], axis=-1)
    pos = jnp.arange(S)
    scale = RET_QK_DIM ** -0.5

    def rot(t):
        return rotary(t.reshape(B, S, RET_HEADS, RET_QK_DIM), pos, RET_QK_DIM, RET_THETA)

    vh = v.reshape(B, S, RET_HEADS, RET_V_DIM)
    log_gamma = jnp.log1p(-jnp.exp2(-decay_exp.astype(F32)))
    fwd = retention_scan(rot(qf), rot(kf) * scale, vh, log_gamma[0], True)
    bwd = retention_scan(rot(qb)[:, ::-1], (rot(kb) * scale)[:, ::-1], vh[:, ::-1], log_gamma[1], False)[:, ::-1]
    y = fwd + bwd
    y = y * lax.rsqrt(jnp.mean(y * y, axis=-1, keepdims=True) + EPS)
    y = y.reshape(B, S, RET_V).astype(h.dtype) * jax.nn.silu(gate)
    return y @ w_out


def setup_inputs(seed: int = 0) -> dict:
    key = jax.random.key(seed)
    ks = jax.random.split(key, 20)
    n_g = len(ATT_GROUPS)

    def dense(k, shape, fan_in):
        return jax.random.normal(k, shape, F32) * (fan_in ** -0.5)

    def gain(k, shape):
        return 1.0 + 0.02 * jax.random.normal(k, shape, F32)

    x = jax.random.normal(ks[0], (BATCH, SEQ, D_MODEL), F32)
    pool_norm = gain(ks[1], (N_POOL_LAYERS, D_MODEL))
    pool_w_in = dense(ks[2], (N_POOL_LAYERS, D_MODEL, 2 * POOL_WIDTH), D_MODEL)
    pool_w_group = dense(ks[3], (N_POOL_LAYERS, len(POOL_WINDOWS), POOL_GROUP, POOL_GROUP), POOL_GROUP)
    pool_scale = gain(ks[4], (N_POOL_LAYERS, POOL_WIDTH))
    pool_w_out = dense(ks[5], (N_POOL_LAYERS, POOL_WIDTH, D_MODEL), POOL_WIDTH)
    att_norm = gain(ks[6], (N_ATT_LAYERS, D_MODEL))
    att_w_in = dense(ks[7], (N_ATT_LAYERS, D_MODEL, 3 * n_g * ATT_WIDTH + ATT_WIDTH), D_MODEL)
    att_q_norm = gain(ks[8], (N_ATT_LAYERS, n_g, ATT_HEAD_DIM))
    att_k_norm = gain(ks[9], (N_ATT_LAYERS, n_g, ATT_HEAD_DIM))
    att_w_out = dense(ks[10], (N_ATT_LAYERS, ATT_WIDTH, D_MODEL), ATT_WIDTH)
    ret_norm = gain(ks[11], (N_RET_LAYERS, D_MODEL))
    ret_w_in = dense(ks[12], (N_RET_LAYERS, D_MODEL, 4 * RET_QK + 2 * RET_V), D_MODEL)
    ret_decay = (RET_DECAY_BASE + jnp.arange(RET_HEADS, dtype=F32))[None, None, :] \
        + 0.1 * jax.random.normal(ks[13], (N_RET_LAYERS, 2, RET_HEADS), F32)
    ret_w_out = dense(ks[14], (N_RET_LAYERS, RET_V, D_MODEL), RET_V)
    return {"x": x,
            "pool_norm": pool_norm, "pool_w_in": pool_w_in, "pool_w_group": pool_w_group,
            "pool_scale": pool_scale, "pool_w_out": pool_w_out,
            "att_norm": att_norm, "att_w_in": att_w_in, "att_q_norm": att_q_norm,
            "att_k_norm": att_k_norm, "att_w_out": att_w_out,
            "ret_norm": ret_norm, "ret_w_in": ret_w_in, "ret_decay": ret_decay, "ret_w_out": ret_w_out}


def reference(x, pool_norm, pool_w_in, pool_w_group, pool_scale, pool_w_out,
              att_norm, att_w_in, att_q_norm, att_k_norm, att_w_out,
              ret_norm, ret_w_in, ret_decay, ret_w_out):
    for layer in range(DEPTH):
        kind, idx = layer % N_MIXERS, layer // N_MIXERS
        if kind == 0:
            h = rms_norm(x, pool_norm[idx])
            x = x + pool_mixer(h, pool_w_in[idx], pool_w_group[idx], pool_scale[idx], pool_w_out[idx])
        elif kind == 1:
            h = rms_norm(x, att_norm[idx])
            x = x + dilated_attention_mixer(h, att_w_in[idx], att_q_norm[idx], att_k_norm[idx], att_w_out[idx])
        else:
            h = rms_norm(x, ret_norm[idx])
            x = x + retention_mixer(h, ret_w_in[idx], ret_decay[idx], ret_w_out[idx])
    return x
```

```python
import functools
import math

import jax
import jax.numpy as jnp
from jax import lax
from jax.experimental import pallas as pl
from jax.experimental.pallas import tpu as pltpu

F32 = jnp.float32
BF16 = jnp.bfloat16

D_MODEL = 1024
EPS = 1e-6
NEG_BIG = -1e30

POOL_WINDOWS = (2, 4, 8, 16)
POOL_WIDTH = 2 * D_MODEL
POOL_GROUP = POOL_WIDTH // len(POOL_WINDOWS)

ATT_GROUPS = ((128, 1), (512, 4), (2048, 16))
ATT_HEAD_DIM = 128
ATT_HEADS = D_MODEL // ATT_HEAD_DIM
ATT_WIDTH = ATT_HEADS * ATT_HEAD_DIM
ATT_SIDE = 64
ROPE_DIM = ATT_HEAD_DIM // 4
ROPE_THETA = 500000.0

RET_HEADS = 4
RET_QK_DIM = D_MODEL // RET_HEADS
RET_V_DIM = 2 * D_MODEL // RET_HEADS
RET_QK = RET_HEADS * RET_QK_DIM
RET_V = RET_HEADS * RET_V_DIM
RET_CHUNK = 128
RET_THETA = 10000.0

V7X_VMEM_BYTES = 64 * 1024 * 1024
VMEM_LIMIT_BYTES = V7X_VMEM_BYTES - 8 * 1024 * 1024
LANES = 128
BF16_ROWS = 16

POOL_TILE = 512
POOL_HALO = BF16_ROWS
POOL_BLOCK = 128
ATT_TILE = 1024
ATT_QBLOCK = 128
ATT_HEADS_PER_STEP = 2
RET_TILE = 512


def _const_spec(shape):
    nd = len(shape)
    return pl.BlockSpec(shape, lambda *_: (0,) * nd, pipeline_mode=pl.Buffered(1))


def _rms(v, w):
    return v * lax.rsqrt(jnp.mean(v * v, axis=-1, keepdims=True) + EPS) * w


def _silu(g):
    return g * (1.0 / (1.0 + jnp.exp(-g)))


def _dot(a, b):
    return jnp.dot(a, b, preferred_element_type=F32)


def _dot_nt(a, b):
    return lax.dot_general(a, b, (((1,), (1,)), ((), ())), preferred_element_type=F32)


def _dot_tn(a, b):
    return lax.dot_general(a, b, (((0,), (0,)), ((), ())), preferred_element_type=F32)


def _pool_kernel(x_ref, xp_ref, xn_ref, nw_ref, win_ref, wg_ref, sc_ref, wout_ref, band_ref,
                 o_ref, h_s, u_s, ub_s, d_s, y_s, *, ts, seq):
    t = pl.program_id(1)
    nt = pl.num_programs(1)
    nw = nw_ref[...]
    halo = POOL_HALO

    h_s[halo:halo + ts, :] = _rms(x_ref[0], nw).astype(BF16)
    hp = _rms(xp_ref[0], nw)
    h_s[0:halo, :] = jnp.where(t > 0, hp, 0.0).astype(BF16)
    hn = _rms(xn_ref[0], nw)
    h_s[halo + ts:, :] = jnp.where(t < nt - 1, hn, 0.0).astype(BF16)

    row = lax.broadcasted_iota(jnp.int32, (POOL_BLOCK, 1), 0)
    for g, window in enumerate(POOL_WINDOWS):
        cols = slice(g * POOL_GROUP, (g + 1) * POOL_GROUP)
        ug = _dot(h_s[...], win_ref[:, cols])
        u_s[...] = ug
        ub_s[...] = ug.astype(BF16)
        for b in range(ts // POOL_BLOCK):
            r0 = b * POOL_BLOCK
            wsum = _dot(band_ref[g], ub_s[r0:r0 + POOL_BLOCK + 2 * halo, :])
            pos = t * ts + r0 + row
            lo = jnp.maximum(pos - window // 2, 0)
            hi = jnp.minimum(pos - window // 2 + window, seq)
            diff = wsum / (hi - lo).astype(F32) - u_s[halo + r0:halo + r0 + POOL_BLOCK, :]
            d_s[r0:r0 + POOL_BLOCK, :] = diff.astype(BF16)
        mixed = _dot(d_s[...], wg_ref[g]) * sc_ref[:, cols]
        gate = _dot(h_s[halo:halo + ts, :], win_ref[:, POOL_WIDTH + g * POOL_GROUP:
                                                    POOL_WIDTH + (g + 1) * POOL_GROUP])
        y_s[:, cols] = (mixed * _silu(gate)).astype(BF16)
    o_ref[0] = x_ref[0] + _dot(y_s[...], wout_ref[...])


def _pool_band():
    i = jnp.arange(POOL_BLOCK)[:, None]
    j = jnp.arange(POOL_BLOCK + 2 * POOL_HALO)[None, :] - POOL_HALO
    bands = [((j - i >= -(w // 2)) & (j - i < w - w // 2)) for w in POOL_WINDOWS]
    return jnp.stack(bands).astype(BF16)


def _pool_layer(x, norm_w, w_in, w_group, scale, w_out):
    B, S, D = x.shape
    ts, halo = POOL_TILE, POOL_HALO
    nt = S // ts
    hb = ts // halo
    n_hb = S // halo
    kern = functools.partial(_pool_kernel, ts=ts, seq=S)
    return pl.pallas_call(
        kern,
        name="pool_layer",
        grid=(B, nt),
        in_specs=[
            pl.BlockSpec((1, ts, D), lambda b, t: (b, t, 0)),
            pl.BlockSpec((1, halo, D), lambda b, t: (b, jnp.maximum(t * hb - 1, 0), 0)),
            pl.BlockSpec((1, halo, D), lambda b, t: (b, jnp.minimum((t + 1) * hb, n_hb - 1), 0)),
            _const_spec((1, D)),
            _const_spec((D, 2 * POOL_WIDTH)),
            _const_spec((len(POOL_WINDOWS), POOL_GROUP, POOL_GROUP)),
            _const_spec((1, POOL_WIDTH)),
            _const_spec((POOL_WIDTH, D)),
            _const_spec((len(POOL_WINDOWS), POOL_BLOCK, POOL_BLOCK + 2 * halo)),
        ],
        out_specs=pl.BlockSpec((1, ts, D), lambda b, t: (b, t, 0)),
        out_shape=jax.ShapeDtypeStruct((B, S, D), F32),
        scratch_shapes=[
            pltpu.VMEM((ts + 2 * halo, D), BF16),
            pltpu.VMEM((ts + 2 * halo, POOL_GROUP), F32),
            pltpu.VMEM((ts + 2 * halo, POOL_GROUP), BF16),
            pltpu.VMEM((ts, POOL_GROUP), BF16),
            pltpu.VMEM((ts, POOL_WIDTH), BF16),
        ],
        compiler_params=pltpu.CompilerParams(
            dimension_semantics=("parallel", "parallel"), vmem_limit_bytes=VMEM_LIMIT_BYTES),
    )(x, x, x, norm_w.reshape(1, D), w_in.astype(BF16), w_group.astype(BF16),
      scale.reshape(1, POOL_WIDTH), w_out.astype(BF16), _pool_band())


def _att_rope_tables(S):
    half = ROPE_DIM // 2
    inv_freq = 1.0 / (ROPE_THETA ** (jnp.arange(half, dtype=F32) * 2.0 / ROPE_DIM))
    ang = jnp.arange(S, dtype=F32)[:, None] * inv_freq[None, :]
    cos, sin = jnp.cos(ang), jnp.sin(ang)
    rest = ATT_HEAD_DIM - ROPE_DIM
    cos_t = jnp.concatenate([cos, cos, jnp.ones((S, rest), F32)], axis=1)
    sin_lo = jnp.concatenate([-sin, jnp.zeros((S, half + rest), F32)], axis=1)
    sin_hi = jnp.concatenate([jnp.zeros((S, half), F32), sin, jnp.zeros((S, rest), F32)], axis=1)
    return cos_t, sin_lo, sin_hi


def _att_proj_kernel(x_ref, nw_ref, w_ref, qn_ref, kn_ref, cos_ref, slo_ref, shi_ref,
                     q_ref, k_ref, v_ref, h_s, p_s, *, ts, dil):
    half = ROPE_DIM // 2
    h_s[...] = _rms(x_ref[0], nw_ref[...]).astype(BF16)
    cos_t, sin_lo, sin_hi = cos_ref[...], slo_ref[...], shi_ref[...]
    n = ts // dil

    def emit(val, out_ref, head):
        cols = slice(head * ATT_HEAD_DIM, (head + 1) * ATT_HEAD_DIM)
        if dil == 1:
            out_ref[0, 0, :, cols] = val.astype(BF16)
        else:
            p_s[...] = val
            for r in range(dil):
                out_ref[0, r, :, cols] = p_s[pl.ds(r, n, stride=dil), :].astype(BF16)

    def qk_head(val, gain, scale):
        y = _rms(val, gain)
        y = (y * cos_t + pltpu.roll(y, ATT_HEAD_DIM - half, 1) * sin_lo
             + pltpu.roll(y, half, 1) * sin_hi)
        return y * scale if scale != 1.0 else y

    for pair in range(3 * ATT_HEADS // 2):
        proj = _dot(h_s[...], w_ref[:, pair * 2 * ATT_HEAD_DIM:(pair + 1) * 2 * ATT_HEAD_DIM])
        for hh in range(2):
            slab = pair * 2 + hh
            kind, head = divmod(slab, ATT_HEADS)
            val = proj[:, hh * ATT_HEAD_DIM:(hh + 1) * ATT_HEAD_DIM]
            if kind == 0:
                emit(qk_head(val, qn_ref[...], ATT_HEAD_DIM ** -0.5), q_ref, head)
            elif kind == 1:
                emit(qk_head(val, kn_ref[...], 1.0), k_ref, head)
            else:
                emit(val, v_ref, head)


def _att_proj(x, norm_w, w_qkv, q_gain, k_gain, tables, dil):
    B, S, D = x.shape
    ts = ATT_TILE
    n = ts // dil
    kern = functools.partial(_att_proj_kernel, ts=ts, dil=dil)
    tab_spec = pl.BlockSpec((ts, ATT_HEAD_DIM), lambda b, t: (t, 0))
    out_spec = pl.BlockSpec((1, dil, n, ATT_WIDTH), lambda b, t: (b, 0, t, 0))
    out_sds = jax.ShapeDtypeStruct((B, dil, S // dil, ATT_WIDTH), BF16)
    return pl.pallas_call(
        kern,
        name=f"att_proj_d{dil}",
        grid=(B, S // ts),
        in_specs=[
            pl.BlockSpec((1, ts, D), lambda b, t: (b, t, 0)),
            _const_spec((1, D)),
            _const_spec((D, 3 * ATT_WIDTH)),
            _const_spec((1, ATT_HEAD_DIM)),
            _const_spec((1, ATT_HEAD_DIM)),
            tab_spec, tab_spec, tab_spec,
        ],
        out_specs=[out_spec, out_spec, out_spec],
        out_shape=[out_sds, out_sds, out_sds],
        scratch_shapes=[pltpu.VMEM((ts, D), BF16), pltpu.VMEM((ts, ATT_HEAD_DIM), F32)],
        compiler_params=pltpu.CompilerParams(
            dimension_semantics=("parallel", "parallel"), vmem_limit_bytes=VMEM_LIMIT_BYTES),
    )(x, norm_w.reshape(1, D), w_qkv, q_gain.reshape(1, -1), k_gain.reshape(1, -1), *tables)


def _att_core_kernel(*refs, ts, seq):
    n_g = len(ATT_GROUPS)
    grp = [refs[7 * g:7 * g + 7] for g in range(n_g)]
    x_ref, nw_ref, wgate_ref, wout_ref = refs[7 * n_g:7 * n_g + 4]
    o_ref = refs[7 * n_g + 4]
    h_s, og_s, lse_s = refs[7 * n_g + 5:]
    t = pl.program_id(1)
    hp = pl.program_id(2)
    side = ATT_SIDE
    hd = ATT_HEAD_DIM

    @pl.when(hp == 0)
    def _():
        h_s[...] = _rms(x_ref[0], nw_ref[...]).astype(BF16)
        o_ref[0] = x_ref[0]

    for g, (_, dil) in enumerate(ATT_GROUPS):
        q_ref, km_ref, kp_ref, kn_ref, vm_ref, vp_ref, vn_ref = grp[g]
        n = ts // dil
        n_sub = seq // dil
        qb = min(ATT_QBLOCK, n)
        nk = qb + 2 * side
        a = lax.broadcasted_iota(jnp.int32, (qb, nk), 0)
        j = lax.broadcasted_iota(jnp.int32, (qb, nk), 1)
        band_bias = jnp.where((j - a >= 0) & (j - a <= 2 * side), 0.0, NEG_BIG).astype(F32)
        jrow = lax.broadcasted_iota(jnp.int32, (1, nk), 1)

        def window(main_ref, prev_ref, next_ref, r, c0):
            parts = []
            if c0 == 0:
                parts.append(prev_ref[0, r])
                lo = 0
            else:
                lo = c0 - side
            hi = min(c0 + qb + side, n)
            parts.append(main_ref[0, r, lo:hi, :])
            if c0 + qb + side > n:
                parts.append(next_ref[0, r])
            return parts[0] if len(parts) == 1 else jnp.concatenate(parts, axis=0)

        def block(r, c0):
            base = t * n + c0 - side
            key_ok = (jrow + base >= 0) & (jrow + base < n_sub)
            bias = band_bias + jnp.where(key_ok, 0.0, NEG_BIG).astype(F32)
            qv = q_ref[0, r, c0:c0 + qb, :]
            kv = window(km_ref, kp_ref, kn_ref, r, c0)
            vv = window(vm_ref, vp_ref, vn_ref, r, c0)
            for hh in range(ATT_HEADS_PER_STEP):
                cols = slice(hh * hd, (hh + 1) * hd)
                s = _dot_nt(qv[:, cols], kv[:, cols]) + bias
                m = jnp.max(s, axis=-1, keepdims=True)
                p = jnp.exp(s - m)
                den = jnp.sum(p, axis=-1, keepdims=True)
                o = _dot(p.astype(BF16), vv[:, cols]) / den
                lse = jnp.broadcast_to(m + jnp.log(den), (qb, hd))
                if dil == 1:
                    og_s[g, hh, c0:c0 + qb, :] = o
                    lse_s[g, hh, c0:c0 + qb, :] = lse
                else:
                    rows = pl.ds(c0 * dil + r, qb, stride=dil)
                    og_s[g, hh, rows, :] = o
                    lse_s[g, hh, rows, :] = lse

        def residue(r):
            for c0 in range(0, n, qb):
                block(r, c0)

        if dil <= 4:
            for r in range(dil):
                residue(r)
        else:
            def body(r, carry):
                residue(r)
                return carry
            lax.fori_loop(0, dil, body, 0)

    ys = []
    for hh in range(ATT_HEADS_PER_STEP):
        lses = [lse_s[g, hh] for g in range(n_g)]
        m = functools.reduce(jnp.maximum, lses)
        ws = [jnp.exp(l - m) for l in lses]
        den = functools.reduce(lambda u, v: u + v, ws)
        num = functools.reduce(lambda u, v: u + v, [w * og_s[g, hh] for g, w in enumerate(ws)])
        ys.append(num / den)
    gate = _dot(h_s[...], wgate_ref[...])
    y = (jnp.concatenate(ys, axis=1) * _silu(gate)).astype(BF16)
    o_ref[0] += _dot(y, wout_ref[...])


def _att_core(x, norm_w, qkv, w_gate, w_out):
    B, S, D = x.shape
    ts = ATT_TILE
    side = ATT_SIDE
    hw = ATT_HEADS_PER_STEP * ATT_HEAD_DIM
    n_hp = ATT_HEADS // ATT_HEADS_PER_STEP
    in_specs, args = [], []
    for (_, dil), (q, k, v) in zip(ATT_GROUPS, qkv):
        n = ts // dil
        per = n // side
        last = S // dil // side - 1
        main = pl.BlockSpec((1, dil, n, hw), lambda b, t, h: (b, 0, t, h))
        prev = pl.BlockSpec((1, dil, side, hw),
                            lambda b, t, h, per=per: (b, 0, jnp.maximum(t * per - 1, 0), h))
        nxt = pl.BlockSpec((1, dil, side, hw),
                           lambda b, t, h, per=per, last=last: (b, 0, jnp.minimum((t + 1) * per, last), h))
        in_specs += [main, main, prev, nxt, main, prev, nxt]
        args += [q, k, k, k, v, v, v]
    in_specs += [
        pl.BlockSpec((1, ts, D), lambda b, t, h: (b, t, 0)),
        pl.BlockSpec((1, D), lambda b, t, h: (0, 0)),
        pl.BlockSpec((D, hw), lambda b, t, h: (0, h)),
        pl.BlockSpec((hw, D), lambda b, t, h: (h, 0)),
    ]
    args += [x, norm_w.reshape(1, D), w_gate, w_out]
    kern = functools.partial(_att_core_kernel, ts=ts, seq=S)
    n_g = len(ATT_GROUPS)
    return pl.pallas_call(
        kern,
        name="att_core",
        grid=(B, S // ts, n_hp),
        in_specs=in_specs,
        out_specs=pl.BlockSpec((1, ts, D), lambda b, t, h: (b, t, 0)),
        out_shape=jax.ShapeDtypeStruct((B, S, D), F32),
        scratch_shapes=[
            pltpu.VMEM((ts, D), BF16),
            pltpu.VMEM((n_g, ATT_HEADS_PER_STEP, ts, ATT_HEAD_DIM), F32),
            pltpu.VMEM((n_g, ATT_HEADS_PER_STEP, ts, ATT_HEAD_DIM), F32),
        ],
        compiler_params=pltpu.CompilerParams(
            dimension_semantics=("parallel", "parallel", "arbitrary"),
            vmem_limit_bytes=VMEM_LIMIT_BYTES),
    )(*args)


def _att_layer(x, norm_w, w_in, q_norm, k_norm, w_out):
    B, S, D = x.shape
    n_g = len(ATT_GROUPS)
    w_in = w_in.astype(BF16)
    tables = _att_rope_tables(S)
    qkv = []
    for g, (_, dil) in enumerate(ATT_GROUPS):
        w_g = w_in[:, 3 * g * ATT_WIDTH:3 * (g + 1) * ATT_WIDTH]
        qkv.append(_att_proj(x, norm_w, w_g, q_norm[g], k_norm[g], tables, dil))
    w_gate = w_in[:, 3 * n_g * ATT_WIDTH:]
    return _att_core(x, norm_w, qkv, w_gate, w_out.astype(BF16))


def _ret_rope_tables(S):
    half = RET_QK_DIM // 2
    inv_freq = 1.0 / (RET_THETA ** (jnp.arange(half, dtype=F32) * 2.0 / RET_QK_DIM))
    ang = jnp.arange(S, dtype=F32)[:, None] * inv_freq[None, :]
    return jnp.cos(ang), jnp.sin(ang)


def _ret_rotate(val, cos, sin, scale):
    half = RET_QK_DIM // 2
    x1, x2 = val[:, :half], val[:, half:]
    out = jnp.concatenate([x1 * cos - x2 * sin, x2 * cos + x1 * sin], axis=1)
    return out * scale if scale != 1.0 else out


def _ret_chunk(q, k, v, st_ref, lg, reverse):
    C = RET_CHUNK
    i = lax.broadcasted_iota(jnp.int32, (C, C), 0)
    jj = lax.broadcasted_iota(jnp.int32, (C, C), 1)
    col = lax.broadcasted_iota(jnp.int32, (C, 1), 0).astype(F32)
    if reverse:
        dist, keep = jj - i, jj > i
        q_dec = jnp.exp((C - col) * lg)
        k_dec = jnp.exp(col * lg)
    else:
        dist, keep = i - jj, i >= jj
        q_dec = jnp.exp((col + 1.0) * lg)
        k_dec = jnp.exp((C - 1.0 - col) * lg)
    decay = jnp.where(keep, jnp.exp(jnp.where(keep, dist, 0).astype(F32) * lg), 0.0)
    inner = _dot_nt(q.astype(BF16), k.astype(BF16)) * decay
    state = st_ref[...]
    out = _dot(inner.astype(BF16), v) + _dot((q * q_dec).astype(BF16), state.astype(BF16))
    chunk_dec = jnp.exp(jnp.full((1, 1), float(C), F32) * lg)
    st_ref[...] = state * chunk_dec + _dot_tn((k * k_dec).astype(BF16), v)
    return out


def _ret_fwd_kernel(lg_ref, x_ref, nw_ref, w_ref, cos_ref, sin_ref, f_ref, v_ref,
                    h_s, q_s, k_s, st_s, *, ts):
    @pl.when(pl.program_id(1) == 0)
    def _():
        st_s[...] = jnp.zeros_like(st_s)

    h_s[...] = _rms(x_ref[0], nw_ref[...]).astype(BF16)
    cos, sin = cos_ref[...], sin_ref[...]
    for hd in range(RET_HEADS):
        qc = slice(hd * RET_QK_DIM, (hd + 1) * RET_QK_DIM)
        kc = slice(RET_QK + hd * RET_QK_DIM, RET_QK + (hd + 1) * RET_QK_DIM)
        vc = slice(2 * RET_QK + hd * RET_V_DIM, 2 * RET_QK + (hd + 1) * RET_V_DIM)
        oc = slice(hd * RET_V_DIM, (hd + 1) * RET_V_DIM)
        q_s[...] = _ret_rotate(_dot(h_s[...], w_ref[:, qc]), cos, sin, 1.0)
        k_s[...] = _ret_rotate(_dot(h_s[...], w_ref[:, kc]), cos, sin, RET_QK_DIM ** -0.5)
        v_ref[0, :, oc] = _dot(h_s[...], w_ref[:, vc]).astype(BF16)
        lg = lg_ref[0, hd]
        for c in range(ts // RET_CHUNK):
            rows = slice(c * RET_CHUNK, (c + 1) * RET_CHUNK)
            out = _ret_chunk(q_s[rows, :], k_s[rows, :], v_ref[0, rows, oc], st_s.at[hd], lg, False)
            f_ref[0, rows, oc] = out


def _ret_bwd_kernel(lg_ref, x_ref, nw_ref, w_ref, cos_ref, sin_ref, f_ref, v_ref, wout_ref,
                    o_ref, h_s, q_s, k_s, y_s, st_s, *, ts):
    @pl.when(pl.program_id(1) == 0)
    def _():
        st_s[...] = jnp.zeros_like(st_s)

    h_s[...] = _rms(x_ref[0], nw_ref[...]).astype(BF16)
    cos, sin = cos_ref[...], sin_ref[...]
    for hd in range(RET_HEADS):
        qc = slice(hd * RET_QK_DIM, (hd + 1) * RET_QK_DIM)
        kc = slice(RET_QK + hd * RET_QK_DIM, RET_QK + (hd + 1) * RET_QK_DIM)
        gc = slice(2 * RET_QK + hd * RET_V_DIM, 2 * RET_QK + (hd + 1) * RET_V_DIM)
        oc = slice(hd * RET_V_DIM, (hd + 1) * RET_V_DIM)
        q_s[...] = _ret_rotate(_dot(h_s[...], w_ref[:, qc]), cos, sin, 1.0)
        k_s[...] = _ret_rotate(_dot(h_s[...], w_ref[:, kc]), cos, sin, RET_QK_DIM ** -0.5)
        lg = lg_ref[1, hd]
        for c in reversed(range(ts // RET_CHUNK)):
            rows = slice(c * RET_CHUNK, (c + 1) * RET_CHUNK)
            out = _ret_chunk(q_s[rows, :], k_s[rows, :], v_ref[0, rows, oc], st_s.at[hd], lg, True)
            y = out + f_ref[0, rows, oc]
            y = y * lax.rsqrt(jnp.mean(y * y, axis=-1, keepdims=True) + EPS)
            gate = _dot(h_s[rows, :], w_ref[:, gc])
            y_s[rows, oc] = (y * _silu(gate)).astype(BF16)
    o_ref[0] = x_ref[0] + _dot(y_s[...], wout_ref[...])


def _ret_layer(x, norm_w, w_in, decay_exp, w_out):
    B, S, D = x.shape
    ts = RET_TILE
    nt = S // ts
    w_in = w_in.astype(BF16)
    w_fwd = jnp.concatenate([w_in[:, :2 * RET_QK], w_in[:, 4 * RET_QK:4 * RET_QK + RET_V]], axis=1)
    w_bwd = jnp.concatenate([w_in[:, 2 * RET_QK:4 * RET_QK], w_in[:, 4 * RET_QK + RET_V:]], axis=1)
    log_gamma = jnp.log1p(-jnp.exp2(-decay_exp.astype(F32)))
    cos, sin = _ret_rope_tables(S)
    half = RET_QK_DIM // 2
    smem = pl.BlockSpec(memory_space=pltpu.SMEM)
    cparams = pltpu.CompilerParams(
        dimension_semantics=("parallel", "arbitrary"), vmem_limit_bytes=VMEM_LIMIT_BYTES)
    state = pltpu.VMEM((RET_HEADS, RET_QK_DIM, RET_V_DIM), F32)

    fwd, v = pl.pallas_call(
        functools.partial(_ret_fwd_kernel, ts=ts),
        name="ret_fwd",
        grid=(B, nt),
        in_specs=[
            smem,
            pl.BlockSpec((1, ts, D), lambda b, t: (b, t, 0)),
            _const_spec((1, D)),
            _const_spec((D, 2 * RET_QK + RET_V)),
            pl.BlockSpec((ts, half), lambda b, t: (t, 0)),
            pl.BlockSpec((ts, half), lambda b, t: (t, 0)),
        ],
        out_specs=[pl.BlockSpec((1, ts, RET_V), lambda b, t: (b, t, 0)),
                   pl.BlockSpec((1, ts, RET_V), lambda b, t: (b, t, 0))],
        out_shape=[jax.ShapeDtypeStruct((B, S, RET_V), F32),
                   jax.ShapeDtypeStruct((B, S, RET_V), BF16)],
        scratch_shapes=[pltpu.VMEM((ts, D), BF16), pltpu.VMEM((ts, RET_QK_DIM), F32),
                        pltpu.VMEM((ts, RET_QK_DIM), F32), state],
        compiler_params=cparams,
    )(log_gamma, x, norm_w.reshape(1, D), w_fwd, cos, sin)

    rev = lambda b, t: (b, nt - 1 - t, 0)
    return pl.pallas_call(
        functools.partial(_ret_bwd_kernel, ts=ts),
        name="ret_bwd",
        grid=(B, nt),
        in_specs=[
            smem,
            pl.BlockSpec((1, ts, D), rev),
            _const_spec((1, D)),
            _const_spec((D, 2 * RET_QK + RET_V)),
            pl.BlockSpec((ts, half), lambda b, t: (nt - 1 - t, 0)),
            pl.BlockSpec((ts, half), lambda b, t: (nt - 1 - t, 0)),
            pl.BlockSpec((1, ts, RET_V), rev),
            pl.BlockSpec((1, ts, RET_V), rev),
            _const_spec((RET_V, D)),
        ],
        out_specs=pl.BlockSpec((1, ts, D), rev),
        out_shape=jax.ShapeDtypeStruct((B, S, D), F32),
        scratch_shapes=[pltpu.VMEM((ts, D), BF16), pltpu.VMEM((ts, RET_QK_DIM), F32),
                        pltpu.VMEM((ts, RET_QK_DIM), F32), pltpu.VMEM((ts, RET_V), BF16), state],
        compiler_params=cparams,
    )(log_gamma, x, norm_w.reshape(1, D), w_bwd, cos, sin, fwd, v, w_out.astype(BF16))


def kernel(x, pool_norm, pool_w_in, pool_w_group, pool_scale, pool_w_out,
           att_norm, att_w_in, att_q_norm, att_k_norm, att_w_out,
           ret_norm, ret_w_in, ret_decay, ret_w_out):
    depth = pool_norm.shape[0] + att_norm.shape[0] + ret_norm.shape[0]
    for layer in range(depth):
        kind, idx = layer % 3, layer // 3
        if kind == 0:
            x = _pool_layer(x, pool_norm[idx], pool_w_in[idx], pool_w_group[idx], pool_scale[idx],
                            pool_w_out[idx])
        elif kind == 1:
            x = _att_layer(x, att_norm[idx], att_w_in[idx], att_q_norm[idx], att_k_norm[idx],
                           att_w_out[idx])
        else:
            x = _ret_layer(x, ret_norm[idx], ret_w_in[idx], ret_decay[idx], ret_w_out[idx])
    return x
```

```python
import functools
import math

import jax
import jax.numpy as jnp
from jax import lax
from jax.experimental import pallas as pl
from jax.experimental.pallas import tpu as pltpu

F32 = jnp.float32
BF16 = jnp.bfloat16

D_MODEL = 1024
EPS = 1e-6
NEG_BIG = -1e30

POOL_WINDOWS = (2, 4, 8, 16)
POOL_WIDTH = 2 * D_MODEL
POOL_GROUP = POOL_WIDTH // len(POOL_WINDOWS)

ATT_GROUPS = ((128, 1), (512, 4), (2048, 16))
ATT_HEAD_DIM = 128
ATT_HEADS = D_MODEL // ATT_HEAD_DIM
ATT_WIDTH = ATT_HEADS * ATT_HEAD_DIM
ATT_SIDE = 64
ROPE_DIM = ATT_HEAD_DIM // 4
ROPE_THETA = 500000.0

RET_HEADS = 4
RET_QK_DIM = D_MODEL // RET_HEADS
RET_V_DIM = 2 * D_MODEL // RET_HEADS
RET_QK = RET_HEADS * RET_QK_DIM
RET_V = RET_HEADS * RET_V_DIM
RET_CHUNK = 256
RET_THETA = 10000.0

V7X_VMEM_BYTES = 64 * 1024 * 1024
VMEM_LIMIT_BYTES = V7X_VMEM_BYTES - 8 * 1024 * 1024
LANES = 128
BF16_ROWS = 16

POOL_TILE = 512
POOL_HALO = BF16_ROWS
POOL_BLOCK = 128
ATT_PROJ_TILE = 1024
ATT_TILE = 2048
ATT_QBLOCK = 128
ATT_OUT_TILE = 1024
ATT_PAIRS = ATT_HEADS // 2
ATT_PAIR_W = 2 * ATT_HEAD_DIM
RET_TILE = 512


def _const_spec(shape):
    nd = len(shape)
    return pl.BlockSpec(shape, lambda *_: (0,) * nd, pipeline_mode=pl.Buffered(1))


def _rms(v, w):
    return v * lax.rsqrt(jnp.mean(v * v, axis=-1, keepdims=True) + EPS) * w


def _silu(g):
    return g * (1.0 / (1.0 + jnp.exp(-g)))


def _dot(a, b):
    return jnp.dot(a, b, preferred_element_type=F32)


def _dot_nt(a, b):
    return lax.dot_general(a, b, (((1,), (1,)), ((), ())), preferred_element_type=F32)


def _dot_tn(a, b):
    return lax.dot_general(a, b, (((0,), (0,)), ((), ())), preferred_element_type=F32)


def _pool_kernel(x_ref, xp_ref, xn_ref, nw_ref, win_ref, wg_ref, sc_ref, wout_ref, band_ref,
                 o_ref, h_s, u_s, ub_s, d_s, y_s, *, ts, seq):
    t = pl.program_id(1)
    nt = pl.num_programs(1)
    nw = nw_ref[...]
    halo = POOL_HALO

    h_s[halo:halo + ts, :] = _rms(x_ref[0], nw).astype(BF16)
    hp = _rms(xp_ref[0], nw)
    h_s[0:halo, :] = jnp.where(t > 0, hp, 0.0).astype(BF16)
    hn = _rms(xn_ref[0], nw)
    h_s[halo + ts:, :] = jnp.where(t < nt - 1, hn, 0.0).astype(BF16)

    row = lax.broadcasted_iota(jnp.int32, (POOL_BLOCK, 1), 0)
    for g, window in enumerate(POOL_WINDOWS):
        cols = slice(g * POOL_GROUP, (g + 1) * POOL_GROUP)
        ug = _dot(h_s[...], win_ref[:, cols])
        u_s[...] = ug
        ub_s[...] = ug.astype(BF16)
        for b in range(ts // POOL_BLOCK):
            r0 = b * POOL_BLOCK
            wsum = _dot(band_ref[g], ub_s[r0:r0 + POOL_BLOCK + 2 * halo, :])
            pos = t * ts + r0 + row
            lo = jnp.maximum(pos - window // 2, 0)
            hi = jnp.minimum(pos - window // 2 + window, seq)
            diff = wsum / (hi - lo).astype(F32) - u_s[halo + r0:halo + r0 + POOL_BLOCK, :]
            d_s[r0:r0 + POOL_BLOCK, :] = diff.astype(BF16)
        mixed = _dot(d_s[...], wg_ref[g]) * sc_ref[:, cols]
        gate = _dot(h_s[halo:halo + ts, :], win_ref[:, POOL_WIDTH + g * POOL_GROUP:
                                                    POOL_WIDTH + (g + 1) * POOL_GROUP])
        y_s[:, cols] = (mixed * _silu(gate)).astype(BF16)
    o_ref[0] = x_ref[0] + _dot(y_s[...], wout_ref[...])


def _pool_band():
    i = jnp.arange(POOL_BLOCK)[:, None]
    j = jnp.arange(POOL_BLOCK + 2 * POOL_HALO)[None, :] - POOL_HALO
    bands = [((j - i >= -(w // 2)) & (j - i < w - w // 2)) for w in POOL_WINDOWS]
    return jnp.stack(bands).astype(BF16)


def _pool_layer(x, norm_w, w_in, w_group, scale, w_out):
    B, S, D = x.shape
    ts, halo = POOL_TILE, POOL_HALO
    nt = S // ts
    hb = ts // halo
    n_hb = S // halo
    kern = functools.partial(_pool_kernel, ts=ts, seq=S)
    return pl.pallas_call(
        kern,
        name="pool_layer",
        grid=(B, nt),
        in_specs=[
            pl.BlockSpec((1, ts, D), lambda b, t: (b, t, 0)),
            pl.BlockSpec((1, halo, D), lambda b, t: (b, jnp.maximum(t * hb - 1, 0), 0)),
            pl.BlockSpec((1, halo, D), lambda b, t: (b, jnp.minimum((t + 1) * hb, n_hb - 1), 0)),
            _const_spec((1, D)),
            _const_spec((D, 2 * POOL_WIDTH)),
            _const_spec((len(POOL_WINDOWS), POOL_GROUP, POOL_GROUP)),
            _const_spec((1, POOL_WIDTH)),
            _const_spec((POOL_WIDTH, D)),
            _const_spec((len(POOL_WINDOWS), POOL_BLOCK, POOL_BLOCK + 2 * halo)),
        ],
        out_specs=pl.BlockSpec((1, ts, D), lambda b, t: (b, t, 0)),
        out_shape=jax.ShapeDtypeStruct((B, S, D), F32),
        scratch_shapes=[
            pltpu.VMEM((ts + 2 * halo, D), BF16),
            pltpu.VMEM((ts + 2 * halo, POOL_GROUP), F32),
            pltpu.VMEM((ts + 2 * halo, POOL_GROUP), BF16),
            pltpu.VMEM((ts, POOL_GROUP), BF16),
            pltpu.VMEM((ts, POOL_WIDTH), BF16),
        ],
        compiler_params=pltpu.CompilerParams(
            dimension_semantics=("parallel", "parallel"), vmem_limit_bytes=VMEM_LIMIT_BYTES),
    )(x, x, x, norm_w.reshape(1, D), w_in.astype(BF16), w_group.astype(BF16),
      scale.reshape(1, POOL_WIDTH), w_out.astype(BF16), _pool_band())


def _att_rope_tables(S):
    half = ROPE_DIM // 2
    inv_freq = 1.0 / (ROPE_THETA ** (jnp.arange(half, dtype=F32) * 2.0 / ROPE_DIM))
    ang = jnp.arange(S, dtype=F32)[:, None] * inv_freq[None, :]
    cos, sin = jnp.cos(ang), jnp.sin(ang)
    rest = ATT_HEAD_DIM - ROPE_DIM
    cos_t = jnp.concatenate([cos, cos, jnp.ones((S, rest), F32)], axis=1)
    sin_lo = jnp.concatenate([-sin, jnp.zeros((S, half + rest), F32)], axis=1)
    sin_hi = jnp.concatenate([jnp.zeros((S, half), F32), sin, jnp.zeros((S, rest), F32)], axis=1)
    return cos_t, sin_lo, sin_hi


def _att_proj_kernel(x_ref, nw_ref, w_ref, qn_ref, kn_ref, cos_ref, slo_ref, shi_ref,
                     q_ref, k_ref, v_ref, h_s, p_s, *, ts, dil):
    half = ROPE_DIM // 2
    h_s[...] = _rms(x_ref[0], nw_ref[...]).astype(BF16)
    cos_t, sin_lo, sin_hi = cos_ref[...], slo_ref[...], shi_ref[...]
    n = ts // dil

    def emit(val, out_ref, head):
        cols = slice(head * ATT_HEAD_DIM, (head + 1) * ATT_HEAD_DIM)
        if dil == 1:
            out_ref[0, 0, :, cols] = val.astype(BF16)
        else:
            p_s[...] = val
            for r in range(dil):
                out_ref[0, r, :, cols] = p_s[pl.ds(r, n, stride=dil), :].astype(BF16)

    def qk_head(val, gain, scale):
        y = _rms(val, gain)
        y = (y * cos_t + pltpu.roll(y, ATT_HEAD_DIM - half, 1) * sin_lo
             + pltpu.roll(y, half, 1) * sin_hi)
        return y * scale if scale != 1.0 else y

    for pair in range(3 * ATT_HEADS // 2):
        proj = _dot(h_s[...], w_ref[:, pair * 2 * ATT_HEAD_DIM:(pair + 1) * 2 * ATT_HEAD_DIM])
        for hh in range(2):
            slab = pair * 2 + hh
            kind, head = divmod(slab, ATT_HEADS)
            val = proj[:, hh * ATT_HEAD_DIM:(hh + 1) * ATT_HEAD_DIM]
            if kind == 0:
                emit(qk_head(val, qn_ref[...], ATT_HEAD_DIM ** -0.5), q_ref, head)
            elif kind == 1:
                emit(qk_head(val, kn_ref[...], 1.0), k_ref, head)
            else:
                emit(val, v_ref, head)


def _att_proj(x, norm_w, w_qkv, q_gain, k_gain, tables, dil):
    B, S, D = x.shape
    ts = ATT_PROJ_TILE
    n = ts // dil
    kern = functools.partial(_att_proj_kernel, ts=ts, dil=dil)
    tab_spec = pl.BlockSpec((ts, ATT_HEAD_DIM), lambda b, t: (t, 0))
    out_spec = pl.BlockSpec((1, dil, n, ATT_WIDTH), lambda b, t: (b, 0, t, 0))
    out_sds = jax.ShapeDtypeStruct((B, dil, S // dil, ATT_WIDTH), BF16)
    return pl.pallas_call(
        kern,
        name=f"att_proj_d{dil}",
        grid=(B, S // ts),
        in_specs=[
            pl.BlockSpec((1, ts, D), lambda b, t: (b, t, 0)),
            _const_spec((1, D)),
            _const_spec((D, 3 * ATT_WIDTH)),
            _const_spec((1, ATT_HEAD_DIM)),
            _const_spec((1, ATT_HEAD_DIM)),
            tab_spec, tab_spec, tab_spec,
        ],
        out_specs=[out_spec, out_spec, out_spec],
        out_shape=[out_sds, out_sds, out_sds],
        scratch_shapes=[pltpu.VMEM((ts, D), BF16), pltpu.VMEM((ts, ATT_HEAD_DIM), F32)],
        compiler_params=pltpu.CompilerParams(
            dimension_semantics=("parallel", "parallel"), vmem_limit_bytes=VMEM_LIMIT_BYTES),
    )(x, norm_w.reshape(1, D), w_qkv, q_gain.reshape(1, -1), k_gain.reshape(1, -1), *tables)


def _att_core_kernel(*refs, ts, seq):
    n_g = len(ATT_GROUPS)
    grp = [refs[7 * g:7 * g + 7] for g in range(n_g)]
    o_ref = refs[7 * n_g]
    og_s, lse_s = refs[7 * n_g + 1:]
    t = pl.program_id(1)
    side = ATT_SIDE
    hd = ATT_HEAD_DIM
    qb = ATT_QBLOCK
    nk = qb + 2 * side

    for g, (_, dil) in enumerate(ATT_GROUPS):
        q_ref, km_ref, kp_ref, kn_ref, vm_ref, vp_ref, vn_ref = grp[g]
        n = ts // dil
        n_sub = seq // dil

        def window(main_ref, prev_ref, next_ref, r, c0):
            parts = []
            if c0 == 0:
                parts.append(prev_ref[0, r])
                lo = 0
            else:
                lo = c0 - side
            hi = min(c0 + qb + side, n)
            parts.append(main_ref[0, r, lo:hi, :])
            if c0 + qb + side > n:
                parts.append(next_ref[0, r])
            return parts[0] if len(parts) == 1 else jnp.concatenate(parts, axis=0)

        @pl.when(t >= 0)
        def _():
            a = lax.broadcasted_iota(jnp.int32, (qb, nk), 0)
            j = lax.broadcasted_iota(jnp.int32, (qb, nk), 1)
            band_bias = jnp.where((j - a >= 0) & (j - a <= 2 * side), 0.0, NEG_BIG).astype(F32)
            jrow = lax.broadcasted_iota(jnp.int32, (1, nk), 1)
            for r in range(dil):
                for c0 in range(0, n, qb):
                    base = t * n + c0 - side
                    key_ok = (jrow + base >= 0) & (jrow + base < n_sub)
                    bias = band_bias + jnp.where(key_ok, 0.0, NEG_BIG).astype(F32)
                    qv = q_ref[0, r, c0:c0 + qb, :]
                    kv = window(km_ref, kp_ref, kn_ref, r, c0)
                    vv = window(vm_ref, vp_ref, vn_ref, r, c0)
                    for hh in range(2):
                        cols = slice(hh * hd, (hh + 1) * hd)
                        s = _dot_nt(qv[:, cols], kv[:, cols]) + bias
                        m = jnp.max(s, axis=-1, keepdims=True)
                        p = jnp.exp(s - m)
                        den = jnp.sum(p, axis=-1, keepdims=True)
                        o = _dot(p.astype(BF16), vv[:, cols]) / den
                        lse = jnp.broadcast_to(m + jnp.log(den), (qb, hd))
                        if dil == 1:
                            rows = slice(c0, c0 + qb)
                        else:
                            rows = pl.ds(c0 * dil + r, qb, stride=dil)
                        og_s[g, hh, rows, :] = o
                        lse_s[g, hh, rows, :] = lse

    ys = []
    for hh in range(2):
        lses = [lse_s[g, hh] for g in range(n_g)]
        m = functools.reduce(jnp.maximum, lses)
        ws = [jnp.exp(l - m) for l in lses]
        den = functools.reduce(lambda u, v: u + v, ws)
        num = functools.reduce(lambda u, v: u + v, [w * og_s[g, hh] for g, w in enumerate(ws)])
        ys.append(num / den)
    o_ref[0] = jnp.concatenate(ys, axis=1).astype(BF16)


def _att_core(qkv, S):
    B = qkv[0][0].shape[0]
    ts = ATT_TILE
    side = ATT_SIDE
    pw = ATT_PAIR_W
    in_specs, args = [], []
    for (_, dil), (q, k, v) in zip(ATT_GROUPS, qkv):
        n = ts // dil
        per = n // side
        last = S // dil // side - 1
        main = pl.BlockSpec((1, dil, n, pw), lambda b, t, h: (b, 0, t, h))
        prev = pl.BlockSpec((1, dil, side, pw),
                            lambda b, t, h, per=per: (b, 0, jnp.maximum(t * per - 1, 0), h))
        nxt = pl.BlockSpec((1, dil, side, pw),
                           lambda b, t, h, per=per, last=last: (b, 0, jnp.minimum((t + 1) * per, last), h))
        in_specs += [main, main, prev, nxt, main, prev, nxt]
        args += [q, k, k, k, v, v, v]
    kern = functools.partial(_att_core_kernel, ts=ts, seq=S)
    n_g = len(ATT_GROUPS)
    return pl.pallas_call(
        kern,
        name="att_core",
        grid=(B, S // ts, ATT_PAIRS),
        in_specs=in_specs,
        out_specs=pl.BlockSpec((1, ts, pw), lambda b, t, h: (b, t, h)),
        out_shape=jax.ShapeDtypeStruct((B, S, ATT_WIDTH), BF16),
        scratch_shapes=[
            pltpu.VMEM((n_g, 2, ts, ATT_HEAD_DIM), F32),
            pltpu.VMEM((n_g, 2, ts, ATT_HEAD_DIM), F32),
        ],
        compiler_params=pltpu.CompilerParams(
            dimension_semantics=("parallel", "parallel", "parallel"),
            vmem_limit_bytes=VMEM_LIMIT_BYTES),
    )(*args)


def _att_out_kernel(x_ref, a_ref, nw_ref, wgate_ref, wout_ref, o_ref):
    h = _rms(x_ref[0], nw_ref[...]).astype(BF16)
    gate = _dot(h, wgate_ref[...])
    y = (a_ref[0].astype(F32) * _silu(gate)).astype(BF16)
    o_ref[0] = x_ref[0] + _dot(y, wout_ref[...])


def _att_out(x, att, norm_w, w_gate, w_out):
    B, S, D = x.shape
    ts = ATT_OUT_TILE
    return pl.pallas_call(
        _att_out_kernel,
        name="att_out",
        grid=(B, S // ts),
        in_specs=[
            pl.BlockSpec((1, ts, D), lambda b, t: (b, t, 0)),
            pl.BlockSpec((1, ts, ATT_WIDTH), lambda b, t: (b, t, 0)),
            _const_spec((1, D)),
            _const_spec((D, ATT_WIDTH)),
            _const_spec((ATT_WIDTH, D)),
        ],
        out_specs=pl.BlockSpec((1, ts, D), lambda b, t: (b, t, 0)),
        out_shape=jax.ShapeDtypeStruct((B, S, D), F32),
        compiler_params=pltpu.CompilerParams(
            dimension_semantics=("parallel", "parallel"), vmem_limit_bytes=VMEM_LIMIT_BYTES),
    )(x, att, norm_w.reshape(1, D), w_gate, w_out)


def _att_layer(x, norm_w, w_in, q_norm, k_norm, w_out):
    B, S, D = x.shape
    n_g = len(ATT_GROUPS)
    w_in = w_in.astype(BF16)
    tables = _att_rope_tables(S)
    qkv = []
    for g, (_, dil) in enumerate(ATT_GROUPS):
        w_g = w_in[:, 3 * g * ATT_WIDTH:3 * (g + 1) * ATT_WIDTH]
        qkv.append(_att_proj(x, norm_w, w_g, q_norm[g], k_norm[g], tables, dil))
    att = _att_core(qkv, S)
    return _att_out(x, att, norm_w, w_in[:, 3 * n_g * ATT_WIDTH:], w_out.astype(BF16))


def _ret_rope_tables(S):
    half = RET_QK_DIM // 2
    inv_freq = 1.0 / (RET_THETA ** (jnp.arange(half, dtype=F32) * 2.0 / RET_QK_DIM))
    ang = jnp.arange(S, dtype=F32)[:, None] * inv_freq[None, :]
    return jnp.cos(ang), jnp.sin(ang)


def _ret_rotate(val, cos, sin, scale):
    half = RET_QK_DIM // 2
    x1, x2 = val[:, :half], val[:, half:]
    out = jnp.concatenate([x1 * cos - x2 * sin, x2 * cos + x1 * sin], axis=1)
    return out * scale if scale != 1.0 else out


def _ret_chunk(q, k, v, st_ref, lg, reverse):
    C = RET_CHUNK
    i = lax.broadcasted_iota(jnp.int32, (C, C), 0)
    jj = lax.broadcasted_iota(jnp.int32, (C, C), 1)
    col = lax.broadcasted_iota(jnp.int32, (C, 1), 0).astype(F32)
    if reverse:
        dist, keep = jj - i, jj > i
        q_dec = jnp.exp((C - col) * lg)
        k_dec = jnp.exp(col * lg)
    else:
        dist, keep = i - jj, i >= jj
        q_dec = jnp.exp((col + 1.0) * lg)
        k_dec = jnp.exp((C - 1.0 - col) * lg)
    decay = jnp.where(keep, jnp.exp(jnp.where(keep, dist, 0).astype(F32) * lg), 0.0)
    inner = _dot_nt(q.astype(BF16), k.astype(BF16)) * decay
    state = st_ref[...]
    out = _dot(inner.astype(BF16), v) + _dot((q * q_dec).astype(BF16), state.astype(BF16))
    chunk_dec = jnp.exp(jnp.full((1, 1), float(C), F32) * lg)
    st_ref[...] = state * chunk_dec + _dot_tn((k * k_dec).astype(BF16), v)
    return out


def _ret_fwd_kernel(lg_ref, x_ref, nw_ref, w_ref, cos_ref, sin_ref, f_ref, v_ref,
                    h_s, q_s, k_s, st_s, *, ts):
    @pl.when(pl.program_id(1) == 0)
    def _():
        st_s[...] = jnp.zeros_like(st_s)

    h_s[...] = _rms(x_ref[0], nw_ref[...]).astype(BF16)
    cos, sin = cos_ref[...], sin_ref[...]
    for hd in range(RET_HEADS):
        qc = slice(hd * RET_QK_DIM, (hd + 1) * RET_QK_DIM)
        kc = slice(RET_QK + hd * RET_QK_DIM, RET_QK + (hd + 1) * RET_QK_DIM)
        vc = slice(2 * RET_QK + hd * RET_V_DIM, 2 * RET_QK + (hd + 1) * RET_V_DIM)
        oc = slice(hd * RET_V_DIM, (hd + 1) * RET_V_DIM)
        q_s[...] = _ret_rotate(_dot(h_s[...], w_ref[:, qc]), cos, sin, 1.0)
        k_s[...] = _ret_rotate(_dot(h_s[...], w_ref[:, kc]), cos, sin, RET_QK_DIM ** -0.5)
        v_ref[0, :, oc] = _dot(h_s[...], w_ref[:, vc]).astype(BF16)
        lg = lg_ref[0, hd]
        for c in range(ts // RET_CHUNK):
            rows = slice(c * RET_CHUNK, (c + 1) * RET_CHUNK)
            out = _ret_chunk(q_s[rows, :], k_s[rows, :], v_ref[0, rows, oc], st_s.at[hd], lg, False)
            f_ref[0, rows, oc] = out


def _ret_bwd_kernel(lg_ref, x_ref, nw_ref, w_ref, cos_ref, sin_ref, f_ref, v_ref, wout_ref,
                    o_ref, h_s, q_s, k_s, y_s, st_s, *, ts):
    @pl.when(pl.program_id(1) == 0)
    def _():
        st_s[...] = jnp.zeros_like(st_s)

    h_s[...] = _rms(x_ref[0], nw_ref[...]).astype(BF16)
    cos, sin = cos_ref[...], sin_ref[...]
    for hd in range(RET_HEADS):
        qc = slice(hd * RET_QK_DIM, (hd + 1) * RET_QK_DIM)
        kc = slice(RET_QK + hd * RET_QK_DIM, RET_QK + (hd + 1) * RET_QK_DIM)
        gc = slice(2 * RET_QK + hd * RET_V_DIM, 2 * RET_QK + (hd + 1) * RET_V_DIM)
        oc = slice(hd * RET_V_DIM, (hd + 1) * RET_V_DIM)
        q_s[...] = _ret_rotate(_dot(h_s[...], w_ref[:, qc]), cos, sin, 1.0)
        k_s[...] = _ret_rotate(_dot(h_s[...], w_ref[:, kc]), cos, sin, RET_QK_DIM ** -0.5)
        lg = lg_ref[1, hd]
        for c in reversed(range(ts // RET_CHUNK)):
            rows = slice(c * RET_CHUNK, (c + 1) * RET_CHUNK)
            out = _ret_chunk(q_s[rows, :], k_s[rows, :], v_ref[0, rows, oc], st_s.at[hd], lg, True)
            y = out + f_ref[0, rows, oc]
            y = y * lax.rsqrt(jnp.mean(y * y, axis=-1, keepdims=True) + EPS)
            gate = _dot(h_s[rows, :], w_ref[:, gc])
            y_s[rows, oc] = (y * _silu(gate)).astype(BF16)
    o_ref[0] = x_ref[0] + _dot(y_s[...], wout_ref[...])


def _ret_layer(x, norm_w, w_in, decay_exp, w_out):
    B, S, D = x.shape
    ts = RET_TILE
    nt = S // ts
    w_in = w_in.astype(BF16)
    w_fwd = jnp.concatenate([w_in[:, :2 * RET_QK], w_in[:, 4 * RET_QK:4 * RET_QK + RET_V]], axis=1)
    w_bwd = jnp.concatenate([w_in[:, 2 * RET_QK:4 * RET_QK], w_in[:, 4 * RET_QK + RET_V:]], axis=1)
    log_gamma = jnp.log1p(-jnp.exp2(-decay_exp.astype(F32)))
    cos, sin = _ret_rope_tables(S)
    half = RET_QK_DIM // 2
    smem = pl.BlockSpec(memory_space=pltpu.SMEM)
    cparams = pltpu.CompilerParams(
        dimension_semantics=("parallel", "arbitrary"), vmem_limit_bytes=VMEM_LIMIT_BYTES)
    state = pltpu.VMEM((RET_HEADS, RET_QK_DIM, RET_V_DIM), F32)

    fwd, v = pl.pallas_call(
        functools.partial(_ret_fwd_kernel, ts=ts),
        name="ret_fwd",
        grid=(B, nt),
        in_specs=[
            smem,
            pl.BlockSpec((1, ts, D), lambda b, t: (b, t, 0)),
            _const_spec((1, D)),
            _const_spec((D, 2 * RET_QK + RET_V)),
            pl.BlockSpec((ts, half), lambda b, t: (t, 0)),
            pl.BlockSpec((ts, half), lambda b, t: (t, 0)),
        ],
        out_specs=[pl.BlockSpec((1, ts, RET_V), lambda b, t: (b, t, 0)),
                   pl.BlockSpec((1, ts, RET_V), lambda b, t: (b, t, 0))],
        out_shape=[jax.ShapeDtypeStruct((B, S, RET_V), F32),
                   jax.ShapeDtypeStruct((B, S, RET_V), BF16)],
        scratch_shapes=[pltpu.VMEM((ts, D), BF16), pltpu.VMEM((ts, RET_QK_DIM), F32),
                        pltpu.VMEM((ts, RET_QK_DIM), F32), state],
        compiler_params=cparams,
    )(log_gamma, x, norm_w.reshape(1, D), w_fwd, cos, sin)

    rev = lambda b, t: (b, nt - 1 - t, 0)
    return pl.pallas_call(
        functools.partial(_ret_bwd_kernel, ts=ts),
        name="ret_bwd",
        grid=(B, nt),
        in_specs=[
            smem,
            pl.BlockSpec((1, ts, D), rev),
            _const_spec((1, D)),
            _const_spec((D, 2 * RET_QK + RET_V)),
            pl.BlockSpec((ts, half), lambda b, t: (nt - 1 - t, 0)),
            pl.BlockSpec((ts, half), lambda b, t: (nt - 1 - t, 0)),
            pl.BlockSpec((1, ts, RET_V), rev),
            pl.BlockSpec((1, ts, RET_V), rev),
            _const_spec((RET_V, D)),
        ],
        out_specs=pl.BlockSpec((1, ts, D), rev),
        out_shape=jax.ShapeDtypeStruct((B, S, D), F32),
        scratch_shapes=[pltpu.VMEM((ts, D), BF16), pltpu.VMEM((ts, RET_QK_DIM), F32),
                        pltpu.VMEM((ts, RET_QK_DIM), F32), pltpu.VMEM((ts, RET_V), BF16), state],
        compiler_params=cparams,
    )(log_gamma, x, norm_w.reshape(1, D), w_bwd, cos, sin, fwd, v, w_out.astype(BF16))


def kernel(x, pool_norm, pool_w_in, pool_w_group, pool_scale, pool_w_out,
           att_norm, att_w_in, att_q_norm, att_k_norm, att_w_out,
           ret_norm, ret_w_in, ret_decay, ret_w_out):
    depth = pool_norm.shape[0] + att_norm.shape[0] + ret_norm.shape[0]
    for layer in range(depth):
        kind, idx = layer % 3, layer // 3
        if kind == 0:
            x = _pool_layer(x, pool_norm[idx], pool_w_in[idx], pool_w_group[idx], pool_scale[idx],
                            pool_w_out[idx])
        elif kind == 1:
            x = _att_layer(x, att_norm[idx], att_w_in[idx], att_q_norm[idx], att_k_norm[idx],
                           att_w_out[idx])
        else:
            x = _ret_layer(x, ret_norm[idx], ret_w_in[idx], ret_decay[idx], ret_w_out[idx])
    return x
```

```python
import functools
import math

import jax
import jax.numpy as jnp
import numpy as np
from jax import lax
from jax.experimental import pallas as pl
from jax.experimental.pallas import tpu as pltpu

F32 = jnp.float32
BF16 = jnp.bfloat16

D_MODEL = 1024
EPS = 1e-6
NEG_BIG = -1e30

POOL_WINDOWS = (2, 4, 8, 16)
POOL_WIDTH = 2 * D_MODEL
POOL_GROUP = POOL_WIDTH // len(POOL_WINDOWS)

ATT_GROUPS = ((128, 1), (512, 4), (2048, 16))
ATT_HEAD_DIM = 128
ATT_HEADS = D_MODEL // ATT_HEAD_DIM
ATT_WIDTH = ATT_HEADS * ATT_HEAD_DIM
ATT_SIDE = 64
ROPE_DIM = ATT_HEAD_DIM // 4
ROPE_THETA = 500000.0

RET_HEADS = 4
RET_QK_DIM = D_MODEL // RET_HEADS
RET_V_DIM = 2 * D_MODEL // RET_HEADS
RET_QK = RET_HEADS * RET_QK_DIM
RET_V = RET_HEADS * RET_V_DIM
RET_CHUNK = 256
RET_THETA = 10000.0

V7X_VMEM_BYTES = 64 * 1024 * 1024
VMEM_LIMIT_BYTES = V7X_VMEM_BYTES - 8 * 1024 * 1024
LANES = 128
BF16_ROWS = 16

POOL_TILE = 512
POOL_HALO = BF16_ROWS
POOL_BLOCK = 128
ATT_PROJ_TILE = 1024
ATT_TILE = 2048
ATT_QBLOCK = 128
ATT_OUT_TILE = 1024
ATT_PAIRS = ATT_HEADS // 2
ATT_PAIR_W = 2 * ATT_HEAD_DIM
RET_TILE = 512


def _const_spec(shape):
    nd = len(shape)
    return pl.BlockSpec(shape, lambda *_: (0,) * nd, pipeline_mode=pl.Buffered(1))


def _rms(v, w):
    return v * lax.rsqrt(jnp.mean(v * v, axis=-1, keepdims=True) + EPS) * w


def _silu(g):
    return g * (1.0 / (1.0 + jnp.exp(-g)))


def _dot(a, b):
    return jnp.dot(a, b, preferred_element_type=F32)


def _dot_nt(a, b):
    return lax.dot_general(a, b, (((1,), (1,)), ((), ())), preferred_element_type=F32)


def _dot_tn(a, b):
    return lax.dot_general(a, b, (((0,), (0,)), ((), ())), preferred_element_type=F32)


def _pool_kernel(x_ref, xp_ref, xn_ref, nw_ref, win_ref, wg_ref, sc_ref, wout_ref, band_ref,
                 o_ref, h_s, u_s, ub_s, d_s, y_s, *, ts, seq):
    t = pl.program_id(1)
    nt = pl.num_programs(1)
    nw = nw_ref[...]
    halo = POOL_HALO
    n_grp = len(POOL_WINDOWS)

    h_s[halo:halo + ts, :] = _rms(x_ref[0], nw).astype(BF16)
    hp = _rms(xp_ref[0], nw)
    h_s[0:halo, :] = jnp.where(t > 0, hp, 0.0).astype(BF16)
    hn = _rms(xn_ref[0], nw)
    h_s[halo + ts:, :] = jnp.where(t < nt - 1, hn, 0.0).astype(BF16)

    row = lax.broadcasted_iota(jnp.int32, (POOL_BLOCK, 1), 0)

    def project(g):
        ug = _dot(h_s[...], win_ref[:, g * POOL_GROUP:(g + 1) * POOL_GROUP])
        u_s[g] = ug
        ub_s[g] = ug.astype(BF16)

    def pool(g):
        window = POOL_WINDOWS[g]
        for b in range(ts // POOL_BLOCK):
            r0 = b * POOL_BLOCK
            wsum = _dot(band_ref[g], ub_s[g, r0:r0 + POOL_BLOCK + 2 * halo, :])
            pos = t * ts + r0 + row
            lo = jnp.maximum(pos - window // 2, 0)
            hi = jnp.minimum(pos - window // 2 + window, seq)
            diff = wsum / (hi - lo).astype(F32) - u_s[g, halo + r0:halo + r0 + POOL_BLOCK, :]
            d_s[g, r0:r0 + POOL_BLOCK, :] = diff.astype(BF16)

    def mix(g):
        cols = slice(g * POOL_GROUP, (g + 1) * POOL_GROUP)
        gate = _dot(h_s[halo:halo + ts, :], win_ref[:, POOL_WIDTH + g * POOL_GROUP:
                                                    POOL_WIDTH + (g + 1) * POOL_GROUP])
        mixed = _dot(d_s[g], wg_ref[g]) * sc_ref[:, cols]
        y_s[:, cols] = (mixed * _silu(gate)).astype(BF16)

    for step in range(n_grp + 2):
        if step < n_grp:
            project(step)
        if 0 <= step - 1 < n_grp:
            pool(step - 1)
        if 0 <= step - 2 < n_grp:
            mix(step - 2)
    o_ref[0] = x_ref[0] + _dot(y_s[...], wout_ref[...])


def _pool_band():
    i = jnp.arange(POOL_BLOCK)[:, None]
    j = jnp.arange(POOL_BLOCK + 2 * POOL_HALO)[None, :] - POOL_HALO
    bands = [((j - i >= -(w // 2)) & (j - i < w - w // 2)) for w in POOL_WINDOWS]
    return jnp.stack(bands).astype(BF16)


def _pool_layer(x, norm_w, w_in, w_group, scale, w_out):
    B, S, D = x.shape
    ts, halo = POOL_TILE, POOL_HALO
    nt = S // ts
    hb = ts // halo
    n_hb = S // halo
    kern = functools.partial(_pool_kernel, ts=ts, seq=S)
    return pl.pallas_call(
        kern,
        name="pool_layer",
        grid=(B, nt),
        in_specs=[
            pl.BlockSpec((1, ts, D), lambda b, t: (b, t, 0)),
            pl.BlockSpec((1, halo, D), lambda b, t: (b, jnp.maximum(t * hb - 1, 0), 0)),
            pl.BlockSpec((1, halo, D), lambda b, t: (b, jnp.minimum((t + 1) * hb, n_hb - 1), 0)),
            _const_spec((1, D)),
            _const_spec((D, 2 * POOL_WIDTH)),
            _const_spec((len(POOL_WINDOWS), POOL_GROUP, POOL_GROUP)),
            _const_spec((1, POOL_WIDTH)),
            _const_spec((POOL_WIDTH, D)),
            _const_spec((len(POOL_WINDOWS), POOL_BLOCK, POOL_BLOCK + 2 * halo)),
        ],
        out_specs=pl.BlockSpec((1, ts, D), lambda b, t: (b, t, 0)),
        out_shape=jax.ShapeDtypeStruct((B, S, D), F32),
        scratch_shapes=[
            pltpu.VMEM((ts + 2 * halo, D), BF16),
            pltpu.VMEM((len(POOL_WINDOWS), ts + 2 * halo, POOL_GROUP), F32),
            pltpu.VMEM((len(POOL_WINDOWS), ts + 2 * halo, POOL_GROUP), BF16),
            pltpu.VMEM((len(POOL_WINDOWS), ts, POOL_GROUP), BF16),
            pltpu.VMEM((ts, POOL_WIDTH), BF16),
        ],
        compiler_params=pltpu.CompilerParams(
            dimension_semantics=("parallel", "parallel"), vmem_limit_bytes=VMEM_LIMIT_BYTES),
    )(x, x, x, norm_w.reshape(1, D), w_in.astype(BF16), w_group.astype(BF16),
      scale.reshape(1, POOL_WIDTH), w_out.astype(BF16), _pool_band())


def _att_head_perm():
    half = ROPE_DIM // 2
    rest = (ATT_HEAD_DIM - ROPE_DIM) // 2
    order = (list(range(half)) + list(range(ROPE_DIM, ROPE_DIM + rest))
             + list(range(half, ROPE_DIM)) + list(range(ROPE_DIM + rest, ATT_HEAD_DIM)))
    return np.array(order, dtype=np.int32)


def _att_rope_tables(S):
    half = ROPE_DIM // 2
    inv_freq = 1.0 / (ROPE_THETA ** (jnp.arange(half, dtype=F32) * 2.0 / ROPE_DIM))
    ang = jnp.arange(S, dtype=F32)[:, None] * inv_freq[None, :]
    pad = ATT_HEAD_DIM // 2 - half
    cos, sin = jnp.cos(ang), jnp.sin(ang)
    one, zero = jnp.ones((S, pad), F32), jnp.zeros((S, pad), F32)
    return (jnp.concatenate([cos, one, cos, one], axis=1),
            jnp.concatenate([-sin, zero, sin, zero], axis=1))


def _att_proj_kernel(x_ref, nw_ref, w_ref, qn_ref, kn_ref, hsum_ref, cos_ref, sin_ref,
                     q_ref, k_ref, v_ref, h_s, p_s, *, ts, dil):
    h_s[...] = _rms(x_ref[0], nw_ref[...]).astype(BF16)
    cos_t, sin_t = cos_ref[...], sin_ref[...]
    n = ts // dil

    def emit(val, out_ref, head):
        cols = slice(head * ATT_HEAD_DIM, (head + 1) * ATT_HEAD_DIM)
        if dil == 1:
            out_ref[0, 0, :, cols] = val.astype(BF16)
        else:
            p_s[...] = val
            for r in range(dil):
                out_ref[0, r, :, cols] = p_s[pl.ds(r, n, stride=dil), :].astype(BF16)

    def qk_head(val, inv_rms, gain):
        y = val * inv_rms * gain
        return y * cos_t + pltpu.roll(y, ATT_HEAD_DIM // 2, 1) * sin_t

    gains = (qn_ref[...] * ATT_HEAD_DIM ** -0.5, kn_ref[...])
    n_pairs = 3 * ATT_HEADS // 2

    def project(pair):
        return _dot(h_s[...], w_ref[:, pair * 2 * ATT_HEAD_DIM:(pair + 1) * 2 * ATT_HEAD_DIM])

    nxt = project(0)
    for pair in range(n_pairs):
        proj = nxt
        if pair + 1 < n_pairs:
            nxt = project(pair + 1)
        kind = pair * 2 // ATT_HEADS
        if kind < 2:
            ssq = _dot((proj * proj).astype(BF16), hsum_ref[...])
            inv_rms = lax.rsqrt(ssq * (1.0 / ATT_HEAD_DIM) + EPS)
        for hh in range(2):
            head = (pair * 2 + hh) % ATT_HEADS
            lanes = slice(hh * ATT_HEAD_DIM, (hh + 1) * ATT_HEAD_DIM)
            if kind < 2:
                out_ref = q_ref if kind == 0 else k_ref
                emit(qk_head(proj[:, lanes], inv_rms[:, lanes], gains[kind]), out_ref, head)
            else:
                emit(proj[:, lanes], v_ref, head)


def _att_proj(x, norm_w, w_qkv, q_gain, k_gain, tables, dil):
    B, S, D = x.shape
    ts = ATT_PROJ_TILE
    n = ts // dil
    kern = functools.partial(_att_proj_kernel, ts=ts, dil=dil)
    tab_spec = pl.BlockSpec((ts, ATT_HEAD_DIM), lambda b, t: (t, 0))
    out_spec = pl.BlockSpec((1, dil, n, ATT_WIDTH), lambda b, t: (b, 0, t, 0))
    out_sds = jax.ShapeDtypeStruct((B, dil, S // dil, ATT_WIDTH), BF16)
    head_sum = jnp.kron(jnp.eye(2, dtype=F32), jnp.ones((ATT_HEAD_DIM, ATT_HEAD_DIM), F32)).astype(BF16)
    return pl.pallas_call(
        kern,
        name=f"att_proj_d{dil}",
        grid=(B, S // ts),
        in_specs=[
            pl.BlockSpec((1, ts, D), lambda b, t: (b, t, 0)),
            _const_spec((1, D)),
            _const_spec((D, 3 * ATT_WIDTH)),
            _const_spec((1, ATT_HEAD_DIM)),
            _const_spec((1, ATT_HEAD_DIM)),
            _const_spec((ATT_PAIR_W, ATT_PAIR_W)),
            tab_spec, tab_spec,
        ],
        out_specs=[out_spec, out_spec, out_spec],
        out_shape=[out_sds, out_sds, out_sds],
        scratch_shapes=[pltpu.VMEM((ts, D), BF16), pltpu.VMEM((ts, ATT_HEAD_DIM), F32)],
        compiler_params=pltpu.CompilerParams(
            dimension_semantics=("parallel", "parallel"), vmem_limit_bytes=VMEM_LIMIT_BYTES),
    )(x, norm_w.reshape(1, D), w_qkv, q_gain.reshape(1, -1), k_gain.reshape(1, -1), head_sum,
      *tables)


def _att_core_kernel(*refs, ts, seq):
    n_g = len(ATT_GROUPS)
    grp = [refs[7 * g:7 * g + 7] for g in range(n_g)]
    o_ref = refs[7 * n_g]
    og_s, lse_s = refs[7 * n_g + 1:]
    t = pl.program_id(1)
    side = ATT_SIDE
    hd = ATT_HEAD_DIM
    qb = ATT_QBLOCK
    nk = qb + 2 * side

    for g, (_, dil) in enumerate(ATT_GROUPS):
        q_ref, km_ref, kp_ref, kn_ref, vm_ref, vp_ref, vn_ref = grp[g]
        n = ts // dil
        n_sub = seq // dil

        def window(main_ref, prev_ref, next_ref, r, c0):
            parts = []
            if c0 == 0:
                parts.append(prev_ref[0, r])
                lo = 0
            else:
                lo = c0 - side
            hi = min(c0 + qb + side, n)
            parts.append(main_ref[0, r, lo:hi, :])
            if c0 + qb + side > n:
                parts.append(next_ref[0, r])
            return parts[0] if len(parts) == 1 else jnp.concatenate(parts, axis=0)

        @pl.when(t >= 0)
        def _():
            a = lax.broadcasted_iota(jnp.int32, (qb, nk), 0)
            j = lax.broadcasted_iota(jnp.int32, (qb, nk), 1)
            band_bias = jnp.where((j - a >= 0) & (j - a <= 2 * side), 0.0, NEG_BIG).astype(F32)
            jrow = lax.broadcasted_iota(jnp.int32, (1, nk), 1)
            for r in range(dil):
                for c0 in range(0, n, qb):
                    base = t * n + c0 - side
                    key_ok = (jrow + base >= 0) & (jrow + base < n_sub)
                    bias = band_bias + jnp.where(key_ok, 0.0, NEG_BIG).astype(F32)
                    qv = q_ref[0, r, c0:c0 + qb, :]
                    kv = window(km_ref, kp_ref, kn_ref, r, c0)
                    vv = window(vm_ref, vp_ref, vn_ref, r, c0)
                    for hh in range(2):
                        cols = slice(hh * hd, (hh + 1) * hd)
                        s = _dot_nt(qv[:, cols], kv[:, cols]) + bias
                        m = jnp.max(s, axis=-1, keepdims=True)
                        p = jnp.exp(s - m)
                        den = jnp.sum(p, axis=-1, keepdims=True)
                        o = _dot(p.astype(BF16), vv[:, cols]) / den
                        lse = jnp.broadcast_to(m + jnp.log(den), (qb, hd))
                        if dil == 1:
                            rows = slice(c0, c0 + qb)
                        else:
                            rows = pl.ds(c0 * dil + r, qb, stride=dil)
                        og_s[g, hh, rows, :] = o
                        lse_s[g, hh, rows, :] = lse

    ys = []
    for hh in range(2):
        lses = [lse_s[g, hh] for g in range(n_g)]
        m = functools.reduce(jnp.maximum, lses)
        ws = [jnp.exp(l - m) for l in lses]
        den = functools.reduce(lambda u, v: u + v, ws)
        num = functools.reduce(lambda u, v: u + v, [w * og_s[g, hh] for g, w in enumerate(ws)])
        ys.append(num / den)
    o_ref[0] = jnp.concatenate(ys, axis=1).astype(BF16)


def _att_core(qkv, S):
    B = qkv[0][0].shape[0]
    ts = ATT_TILE
    side = ATT_SIDE
    pw = ATT_PAIR_W
    in_specs, args = [], []
    for (_, dil), (q, k, v) in zip(ATT_GROUPS, qkv):
        n = ts // dil
        per = n // side
        last = S // dil // side - 1
        main = pl.BlockSpec((1, dil, n, pw), lambda b, t, h: (b, 0, t, h))
        prev = pl.BlockSpec((1, dil, side, pw),
                            lambda b, t, h, per=per: (b, 0, jnp.maximum(t * per - 1, 0), h))
        nxt = pl.BlockSpec((1, dil, side, pw),
                           lambda b, t, h, per=per, last=last: (b, 0, jnp.minimum((t + 1) * per, last), h))
        in_specs += [main, main, prev, nxt, main, prev, nxt]
        args += [q, k, k, k, v, v, v]
    kern = functools.partial(_att_core_kernel, ts=ts, seq=S)
    n_g = len(ATT_GROUPS)
    return pl.pallas_call(
        kern,
        name="att_core",
        grid=(B, S // ts, ATT_PAIRS),
        in_specs=in_specs,
        out_specs=pl.BlockSpec((1, ts, pw), lambda b, t, h: (b, t, h)),
        out_shape=jax.ShapeDtypeStruct((B, S, ATT_WIDTH), BF16),
        scratch_shapes=[
            pltpu.VMEM((n_g, 2, ts, ATT_HEAD_DIM), F32),
            pltpu.VMEM((n_g, 2, ts, ATT_HEAD_DIM), F32),
        ],
        compiler_params=pltpu.CompilerParams(
            dimension_semantics=("parallel", "parallel", "parallel"),
            vmem_limit_bytes=VMEM_LIMIT_BYTES),
    )(*args)


def _att_out_kernel(x_ref, a_ref, nw_ref, wgate_ref, wout_ref, o_ref):
    h = _rms(x_ref[0], nw_ref[...]).astype(BF16)
    gate = _dot(h, wgate_ref[...])
    y = (a_ref[0].astype(F32) * _silu(gate)).astype(BF16)
    o_ref[0] = x_ref[0] + _dot(y, wout_ref[...])


def _att_out(x, att, norm_w, w_gate, w_out):
    B, S, D = x.shape
    ts = ATT_OUT_TILE
    return pl.pallas_call(
        _att_out_kernel,
        name="att_out",
        grid=(B, S // ts),
        in_specs=[
            pl.BlockSpec((1, ts, D), lambda b, t: (b, t, 0)),
            pl.BlockSpec((1, ts, ATT_WIDTH), lambda b, t: (b, t, 0)),
            _const_spec((1, D)),
            _const_spec((D, ATT_WIDTH)),
            _const_spec((ATT_WIDTH, D)),
        ],
        out_specs=pl.BlockSpec((1, ts, D), lambda b, t: (b, t, 0)),
        out_shape=jax.ShapeDtypeStruct((B, S, D), F32),
        compiler_params=pltpu.CompilerParams(
            dimension_semantics=("parallel", "parallel"), vmem_limit_bytes=VMEM_LIMIT_BYTES),
    )(x, att, norm_w.reshape(1, D), w_gate, w_out)


def _att_layer(x, norm_w, w_in, q_norm, k_norm, w_out):
    B, S, D = x.shape
    n_g = len(ATT_GROUPS)
    w_in = w_in.astype(BF16)
    tables = _att_rope_tables(S)
    perm = _att_head_perm()
    qk_cols = (jnp.arange(2 * ATT_WIDTH).reshape(-1, ATT_HEAD_DIM)[:, perm]).reshape(-1)
    qkv = []
    for g, (_, dil) in enumerate(ATT_GROUPS):
        w_g = w_in[:, 3 * g * ATT_WIDTH:3 * (g + 1) * ATT_WIDTH]
        w_g = jnp.concatenate([w_g[:, qk_cols], w_g[:, 2 * ATT_WIDTH:]], axis=1)
        qkv.append(_att_proj(x, norm_w, w_g, q_norm[g][perm], k_norm[g][perm], tables, dil))
    att = _att_core(qkv, S)
    return _att_out(x, att, norm_w, w_in[:, 3 * n_g * ATT_WIDTH:], w_out.astype(BF16))


def _ret_rope_tables(S):
    half = RET_QK_DIM // 2
    inv_freq = 1.0 / (RET_THETA ** (jnp.arange(half, dtype=F32) * 2.0 / RET_QK_DIM))
    ang = jnp.arange(S, dtype=F32)[:, None] * inv_freq[None, :]
    return jnp.cos(ang), jnp.sin(ang)


def _ret_rotate(val, cos, sin, scale):
    half = RET_QK_DIM // 2
    x1, x2 = val[:, :half], val[:, half:]
    out = jnp.concatenate([x1 * cos - x2 * sin, x2 * cos + x1 * sin], axis=1)
    return out * scale if scale != 1.0 else out


def _ret_decays(lg, reverse):
    C = RET_CHUNK
    i = lax.broadcasted_iota(jnp.int32, (C, C), 0)
    jj = lax.broadcasted_iota(jnp.int32, (C, C), 1)
    col = lax.broadcasted_iota(jnp.int32, (C, 1), 0).astype(F32)
    if reverse:
        dist, keep = jj - i, jj > i
        q_dec = jnp.exp((C - col) * lg)
        k_dec = jnp.exp(col * lg)
    else:
        dist, keep = i - jj, i >= jj
        q_dec = jnp.exp((col + 1.0) * lg)
        k_dec = jnp.exp((C - 1.0 - col) * lg)
    decay = jnp.where(keep, jnp.exp(jnp.where(keep, dist, 0).astype(F32) * lg), 0.0)
    chunk_dec = jnp.exp(jnp.full((1, 1), float(C), F32) * lg)
    return decay, q_dec, k_dec, chunk_dec


def _ret_project_qk(h_s, w_ref, cos_ref, sin_ref, q_s, k_s):
    cos, sin = cos_ref[...], sin_ref[...]
    for hd in range(RET_HEADS):
        qc = slice(hd * RET_QK_DIM, (hd + 1) * RET_QK_DIM)
        kc = slice(RET_QK + hd * RET_QK_DIM, RET_QK + (hd + 1) * RET_QK_DIM)
        q_s[hd] = _ret_rotate(_dot(h_s[...], w_ref[:, qc]), cos, sin, 1.0)
        k_s[hd] = _ret_rotate(_dot(h_s[...], w_ref[:, kc]), cos, sin, RET_QK_DIM ** -0.5)


def _ret_scan_chunk(rows, lg_row, q_s, k_s, v_at, st_s, reverse, filler=None):
    heads = range(RET_HEADS)
    decs = [_ret_decays(lg_row(hd), reverse) for hd in heads]
    q = [q_s[hd, rows, :] for hd in heads]
    k = [k_s[hd, rows, :] for hd in heads]
    v = [v_at(hd) for hd in heads]
    inner = [_dot_nt(q[hd].astype(BF16), k[hd].astype(BF16)) for hd in heads]
    kv = [_dot_tn((k[hd] * decs[hd][2]).astype(BF16), v[hd]) for hd in heads]
    extra = filler() if filler is not None else None
    outs = []
    for hd in heads:
        decay, q_dec, _, chunk_dec = decs[hd]
        state = st_s[hd]
        outs.append(_dot((inner[hd] * decay).astype(BF16), v[hd])
                    + _dot((q[hd] * q_dec).astype(BF16), state.astype(BF16)))
        st_s[hd] = state * chunk_dec + kv[hd]
    return outs, extra


def _ret_fwd_kernel(lg_ref, x_ref, nw_ref, w_ref, cos_ref, sin_ref, f_ref, v_ref,
                    h_s, q_s, k_s, st_s, *, ts):
    @pl.when(pl.program_id(1) == 0)
    def _():
        st_s[...] = jnp.zeros_like(st_s)

    h_s[...] = _rms(x_ref[0], nw_ref[...]).astype(BF16)
    _ret_project_qk(h_s, w_ref, cos_ref, sin_ref, q_s, k_s)
    for hd in range(RET_HEADS):
        vc = slice(2 * RET_QK + hd * RET_V_DIM, 2 * RET_QK + (hd + 1) * RET_V_DIM)
        v_ref[0, :, hd * RET_V_DIM:(hd + 1) * RET_V_DIM] = _dot(h_s[...], w_ref[:, vc]).astype(BF16)
    for c in range(ts // RET_CHUNK):
        rows = slice(c * RET_CHUNK, (c + 1) * RET_CHUNK)
        outs, _ = _ret_scan_chunk(
            rows, lambda hd: lg_ref[0, hd], q_s, k_s,
            lambda hd: v_ref[0, rows, hd * RET_V_DIM:(hd + 1) * RET_V_DIM], st_s, False)
        for hd in range(RET_HEADS):
            f_ref[0, rows, hd * RET_V_DIM:(hd + 1) * RET_V_DIM] = outs[hd]


def _ret_bwd_kernel(lg_ref, x_ref, nw_ref, w_ref, cos_ref, sin_ref, f_ref, v_ref, wout_ref,
                    o_ref, h_s, q_s, k_s, y_s, st_s, *, ts):
    @pl.when(pl.program_id(1) == 0)
    def _():
        st_s[...] = jnp.zeros_like(st_s)

    h_s[...] = _rms(x_ref[0], nw_ref[...]).astype(BF16)
    _ret_project_qk(h_s, w_ref, cos_ref, sin_ref, q_s, k_s)
    for c in reversed(range(ts // RET_CHUNK)):
        rows = slice(c * RET_CHUNK, (c + 1) * RET_CHUNK)

        def gates():
            return [_dot(h_s[rows, :], w_ref[:, 2 * RET_QK + hd * RET_V_DIM:
                                             2 * RET_QK + (hd + 1) * RET_V_DIM])
                    for hd in range(RET_HEADS)]

        outs, gate = _ret_scan_chunk(
            rows, lambda hd: lg_ref[1, hd], q_s, k_s,
            lambda hd: v_ref[0, rows, hd * RET_V_DIM:(hd + 1) * RET_V_DIM], st_s, True, gates)
        for hd in range(RET_HEADS):
            oc = slice(hd * RET_V_DIM, (hd + 1) * RET_V_DIM)
            y = outs[hd] + f_ref[0, rows, oc]
            y = y * lax.rsqrt(jnp.mean(y * y, axis=-1, keepdims=True) + EPS)
            y_s[rows, oc] = (y * _silu(gate[hd])).astype(BF16)
    o_ref[0] = x_ref[0] + _dot(y_s[...], wout_ref[...])


def _ret_layer(x, norm_w, w_in, decay_exp, w_out):
    B, S, D = x.shape
    ts = RET_TILE
    nt = S // ts
    w_in = w_in.astype(BF16)
    w_fwd = jnp.concatenate([w_in[:, :2 * RET_QK], w_in[:, 4 * RET_QK:4 * RET_QK + RET_V]], axis=1)
    w_bwd = jnp.concatenate([w_in[:, 2 * RET_QK:4 * RET_QK], w_in[:, 4 * RET_QK + RET_V:]], axis=1)
    log_gamma = jnp.log1p(-jnp.exp2(-decay_exp.astype(F32)))
    cos, sin = _ret_rope_tables(S)
    half = RET_QK_DIM // 2
    smem = pl.BlockSpec(memory_space=pltpu.SMEM)
    cparams = pltpu.CompilerParams(
        dimension_semantics=("parallel", "arbitrary"), vmem_limit_bytes=VMEM_LIMIT_BYTES)
    state = pltpu.VMEM((RET_HEADS, RET_QK_DIM, RET_V_DIM), F32)
    qk_scratch = pltpu.VMEM((RET_HEADS, ts, RET_QK_DIM), F32)

    fwd, v = pl.pallas_call(
        functools.partial(_ret_fwd_kernel, ts=ts),
        name="ret_fwd",
        grid=(B, nt),
        in_specs=[
            smem,
            pl.BlockSpec((1, ts, D), lambda b, t: (b, t, 0)),
            _const_spec((1, D)),
            _const_spec((D, 2 * RET_QK + RET_V)),
            pl.BlockSpec((ts, half), lambda b, t: (t, 0)),
            pl.BlockSpec((ts, half), lambda b, t: (t, 0)),
        ],
        out_specs=[pl.BlockSpec((1, ts, RET_V), lambda b, t: (b, t, 0)),
                   pl.BlockSpec((1, ts, RET_V), lambda b, t: (b, t, 0))],
        out_shape=[jax.ShapeDtypeStruct((B, S, RET_V), F32),
                   jax.ShapeDtypeStruct((B, S, RET_V), BF16)],
        scratch_shapes=[pltpu.VMEM((ts, D), BF16), qk_scratch, qk_scratch, state],
        compiler_params=cparams,
    )(log_gamma, x, norm_w.reshape(1, D), w_fwd, cos, sin)

    rev = lambda b, t: (b, nt - 1 - t, 0)
    return pl.pallas_call(
        functools.partial(_ret_bwd_kernel, ts=ts),
        name="ret_bwd",
        grid=(B, nt),
        in_specs=[
            smem,
            pl.BlockSpec((1, ts, D), rev),
            _const_spec((1, D)),
            _const_spec((D, 2 * RET_QK + RET_V)),
            pl.BlockSpec((ts, half), lambda b, t: (nt - 1 - t, 0)),
            pl.BlockSpec((ts, half), lambda b, t: (nt - 1 - t, 0)),
            pl.BlockSpec((1, ts, RET_V), rev),
            pl.BlockSpec((1, ts, RET_V), rev),
            _const_spec((RET_V, D)),
        ],
        out_specs=pl.BlockSpec((1, ts, D), rev),
        out_shape=jax.ShapeDtypeStruct((B, S, D), F32),
        scratch_shapes=[pltpu.VMEM((ts, D), BF16), qk_scratch, qk_scratch,
                        pltpu.VMEM((ts, RET_V), BF16), state],
        compiler_params=cparams,
    )(log_gamma, x, norm_w.reshape(1, D), w_bwd, cos, sin, fwd, v, w_out.astype(BF16))


def kernel(x, pool_norm, pool_w_in, pool_w_group, pool_scale, pool_w_out,
           att_norm, att_w_in, att_q_norm, att_k_norm, att_w_out,
           ret_norm, ret_w_in, ret_decay, ret_w_out):
    depth = pool_norm.shape[0] + att_norm.shape[0] + ret_norm.shape[0]
    for layer in range(depth):
        kind, idx = layer % 3, layer // 3
        if kind == 0:
            x = _pool_layer(x, pool_norm[idx], pool_w_in[idx], pool_w_group[idx], pool_scale[idx],
                            pool_w_out[idx])
        elif kind == 1:
            x = _att_layer(x, att_norm[idx], att_w_in[idx], att_q_norm[idx], att_k_norm[idx],
                           att_w_out[idx])
        else:
            x = _ret_layer(x, ret_norm[idx], ret_w_in[idx], ret_decay[idx], ret_w_out[idx])
    return x
```

```python
import functools
import math

import jax
import jax.numpy as jnp
import numpy as np
from jax import lax
from jax.experimental import pallas as pl
from jax.experimental.pallas import tpu as pltpu

F32 = jnp.float32
BF16 = jnp.bfloat16

D_MODEL = 1024
EPS = 1e-6
NEG_BIG = -1e30

POOL_WINDOWS = (2, 4, 8, 16)
POOL_WIDTH = 2 * D_MODEL
POOL_GROUP = POOL_WIDTH // len(POOL_WINDOWS)

ATT_GROUPS = ((128, 1), (512, 4), (2048, 16))
ATT_HEAD_DIM = 128
ATT_HEADS = D_MODEL // ATT_HEAD_DIM
ATT_WIDTH = ATT_HEADS * ATT_HEAD_DIM
ATT_SIDE = 64
ROPE_DIM = ATT_HEAD_DIM // 4
ROPE_THETA = 500000.0

RET_HEADS = 4
RET_QK_DIM = D_MODEL // RET_HEADS
RET_V_DIM = 2 * D_MODEL // RET_HEADS
RET_QK = RET_HEADS * RET_QK_DIM
RET_V = RET_HEADS * RET_V_DIM
RET_CHUNK = 256
RET_THETA = 10000.0

V7X_VMEM_BYTES = 64 * 1024 * 1024
VMEM_LIMIT_BYTES = V7X_VMEM_BYTES - 8 * 1024 * 1024
LANES = 128
BF16_ROWS = 16

POOL_TILE = 512
POOL_HALO = BF16_ROWS
POOL_BLOCK = 128
ATT_PROJ_TILE = 1024
ATT_TILE = 2048
ATT_QBLOCK = 128
ATT_OUT_TILE = 1024
ATT_MERGE_ROWS = 32
ATT_PAIRS = ATT_HEADS // 2
ATT_PAIR_W = 2 * ATT_HEAD_DIM
RET_TILE = 512


def _const_spec(shape):
    nd = len(shape)
    return pl.BlockSpec(shape, lambda *_: (0,) * nd, pipeline_mode=pl.Buffered(1))


def _rms(v, w):
    return v * lax.rsqrt(jnp.mean(v * v, axis=-1, keepdims=True) + EPS) * w


def _silu(g):
    return g * (1.0 / (1.0 + jnp.exp(-g)))


def _dot(a, b):
    return jnp.dot(a, b, preferred_element_type=F32)


def _dot_nt(a, b):
    return lax.dot_general(a, b, (((1,), (1,)), ((), ())), preferred_element_type=F32)


def _dot_tn(a, b):
    return lax.dot_general(a, b, (((0,), (0,)), ((), ())), preferred_element_type=F32)


def _pool_kernel(x_ref, xp_ref, xn_ref, nw_ref, win_ref, wg_ref, sc_ref, wout_ref, band_ref,
                 o_ref, h_s, u_s, ub_s, d_s, y_s, *, ts, seq):
    t = pl.program_id(1)
    nt = pl.num_programs(1)
    nw = nw_ref[...]
    halo = POOL_HALO
    n_grp = len(POOL_WINDOWS)

    h_s[halo:halo + ts, :] = _rms(x_ref[0], nw).astype(BF16)
    hp = _rms(xp_ref[0], nw)
    h_s[0:halo, :] = jnp.where(t > 0, hp, 0.0).astype(BF16)
    hn = _rms(xn_ref[0], nw)
    h_s[halo + ts:, :] = jnp.where(t < nt - 1, hn, 0.0).astype(BF16)

    row = lax.broadcasted_iota(jnp.int32, (POOL_BLOCK, 1), 0)

    def project(g):
        ug = _dot(h_s[...], win_ref[:, g * POOL_GROUP:(g + 1) * POOL_GROUP])
        u_s[g] = ug
        ub_s[g] = ug.astype(BF16)

    def pool(g):
        window = POOL_WINDOWS[g]
        for b in range(ts // POOL_BLOCK):
            r0 = b * POOL_BLOCK
            wsum = _dot(band_ref[g], ub_s[g, r0:r0 + POOL_BLOCK + 2 * halo, :])
            pos = t * ts + r0 + row
            lo = jnp.maximum(pos - window // 2, 0)
            hi = jnp.minimum(pos - window // 2 + window, seq)
            diff = wsum / (hi - lo).astype(F32) - u_s[g, halo + r0:halo + r0 + POOL_BLOCK, :]
            d_s[g, r0:r0 + POOL_BLOCK, :] = diff.astype(BF16)

    def mix(g):
        cols = slice(g * POOL_GROUP, (g + 1) * POOL_GROUP)
        gate = _dot(h_s[halo:halo + ts, :], win_ref[:, POOL_WIDTH + g * POOL_GROUP:
                                                    POOL_WIDTH + (g + 1) * POOL_GROUP])
        mixed = _dot(d_s[g], wg_ref[g]) * sc_ref[:, cols]
        y_s[:, cols] = (mixed * _silu(gate)).astype(BF16)

    for step in range(n_grp + 2):
        if step < n_grp:
            project(step)
        if 0 <= step - 1 < n_grp:
            pool(step - 1)
        if 0 <= step - 2 < n_grp:
            mix(step - 2)
    o_ref[0] = x_ref[0] + _dot(y_s[...], wout_ref[...])


def _pool_band():
    i = jnp.arange(POOL_BLOCK)[:, None]
    j = jnp.arange(POOL_BLOCK + 2 * POOL_HALO)[None, :] - POOL_HALO
    bands = [((j - i >= -(w // 2)) & (j - i < w - w // 2)) for w in POOL_WINDOWS]
    return jnp.stack(bands).astype(BF16)


def _pool_layer(x, norm_w, w_in, w_group, scale, w_out):
    B, S, D = x.shape
    ts, halo = POOL_TILE, POOL_HALO
    nt = S // ts
    hb = ts // halo
    n_hb = S // halo
    kern = functools.partial(_pool_kernel, ts=ts, seq=S)
    return pl.pallas_call(
        kern,
        name="pool_layer",
        grid=(B, nt),
        in_specs=[
            pl.BlockSpec((1, ts, D), lambda b, t: (b, t, 0)),
            pl.BlockSpec((1, halo, D), lambda b, t: (b, jnp.maximum(t * hb - 1, 0), 0)),
            pl.BlockSpec((1, halo, D), lambda b, t: (b, jnp.minimum((t + 1) * hb, n_hb - 1), 0)),
            _const_spec((1, D)),
            _const_spec((D, 2 * POOL_WIDTH)),
            _const_spec((len(POOL_WINDOWS), POOL_GROUP, POOL_GROUP)),
            _const_spec((1, POOL_WIDTH)),
            _const_spec((POOL_WIDTH, D)),
            _const_spec((len(POOL_WINDOWS), POOL_BLOCK, POOL_BLOCK + 2 * halo)),
        ],
        out_specs=pl.BlockSpec((1, ts, D), lambda b, t: (b, t, 0)),
        out_shape=jax.ShapeDtypeStruct((B, S, D), F32),
        scratch_shapes=[
            pltpu.VMEM((ts + 2 * halo, D), BF16),
            pltpu.VMEM((len(POOL_WINDOWS), ts + 2 * halo, POOL_GROUP), F32),
            pltpu.VMEM((len(POOL_WINDOWS), ts + 2 * halo, POOL_GROUP), BF16),
            pltpu.VMEM((len(POOL_WINDOWS), ts, POOL_GROUP), BF16),
            pltpu.VMEM((ts, POOL_WIDTH), BF16),
        ],
        compiler_params=pltpu.CompilerParams(
            dimension_semantics=("parallel", "parallel"), vmem_limit_bytes=VMEM_LIMIT_BYTES),
    )(x, x, x, norm_w.reshape(1, D), w_in.astype(BF16), w_group.astype(BF16),
      scale.reshape(1, POOL_WIDTH), w_out.astype(BF16), _pool_band())


def _att_head_perm():
    half = ROPE_DIM // 2
    rest = (ATT_HEAD_DIM - ROPE_DIM) // 2
    order = (list(range(half)) + list(range(ROPE_DIM, ROPE_DIM + rest))
             + list(range(half, ROPE_DIM)) + list(range(ROPE_DIM + rest, ATT_HEAD_DIM)))
    return np.array(order, dtype=np.int32)


def _att_rope_tables(S):
    half = ROPE_DIM // 2
    inv_freq = 1.0 / (ROPE_THETA ** (jnp.arange(half, dtype=F32) * 2.0 / ROPE_DIM))
    ang = jnp.arange(S, dtype=F32)[:, None] * inv_freq[None, :]
    pad = ATT_HEAD_DIM // 2 - half
    cos, sin = jnp.cos(ang), jnp.sin(ang)
    one, zero = jnp.ones((S, pad), F32), jnp.zeros((S, pad), F32)
    return (jnp.concatenate([cos, one, cos, one], axis=1),
            jnp.concatenate([-sin, zero, sin, zero], axis=1))


def _att_proj_kernel(*refs, ts, dil):
    x_refs = refs[:dil]
    nw_ref, w_ref, qn_ref, kn_ref, hsum_ref, cos_ref, sin_ref, q_ref, k_ref, v_ref, h_s = refs[dil:]
    hd = ATT_HEAD_DIM
    n = ts // dil
    for r in range(dil):
        h_s[r * n:(r + 1) * n, :] = _rms(x_refs[r][0], nw_ref[...]).astype(BF16)
    cos_t, sin_t = cos_ref[...], sin_ref[...]

    def emit(val, out_ref, pair, hh):
        out_ref[0, pair, :, :, hh * hd:(hh + 1) * hd] = val.astype(BF16).reshape(dil, n, hd)

    def qk_head(val, inv_rms, gain):
        y = val * inv_rms * gain
        return y * cos_t + pltpu.roll(y, hd // 2, 1) * sin_t

    gains = (qn_ref[...] * hd ** -0.5, kn_ref[...])
    out_refs = (q_ref, k_ref, v_ref)
    n_pairs = 3 * ATT_PAIRS

    def project(i):
        return _dot(h_s[...], w_ref[:, i * ATT_PAIR_W:(i + 1) * ATT_PAIR_W])

    nxt = project(0)
    for i in range(n_pairs):
        proj = nxt
        if i + 1 < n_pairs:
            nxt = project(i + 1)
        kind, pair = divmod(i, ATT_PAIRS)
        if kind < 2:
            ssq = _dot((proj * proj).astype(BF16), hsum_ref[...])
            inv_rms = lax.rsqrt(ssq * (1.0 / hd) + EPS)
        for hh in range(2):
            lanes = slice(hh * hd, (hh + 1) * hd)
            if kind < 2:
                emit(qk_head(proj[:, lanes], inv_rms[:, lanes], gains[kind]), out_refs[kind], pair, hh)
            else:
                emit(proj[:, lanes], v_ref, pair, hh)


def _att_proj(x, norm_w, w_qkv, q_gain, k_gain, tables, dil):
    B, S, D = x.shape
    ts = ATT_PROJ_TILE
    n = ts // dil
    kern = functools.partial(_att_proj_kernel, ts=ts, dil=dil)
    x_view = x.reshape(B, S // dil, dil * D)
    x_specs = [pl.BlockSpec((1, n, D), lambda b, t, r=r: (b, t, r)) for r in range(dil)]
    tables = [tab.reshape(S // ts, n, dil, ATT_HEAD_DIM).transpose(0, 2, 1, 3).reshape(S, ATT_HEAD_DIM)
              for tab in tables]
    tab_spec = pl.BlockSpec((ts, ATT_HEAD_DIM), lambda b, t: (t, 0))
    out_spec = pl.BlockSpec((1, ATT_PAIRS, dil, n, ATT_PAIR_W), lambda b, t: (b, 0, 0, t, 0))
    out_sds = jax.ShapeDtypeStruct((B, ATT_PAIRS, dil, S // dil, ATT_PAIR_W), BF16)
    head_sum = jnp.kron(jnp.eye(2, dtype=F32), jnp.ones((ATT_HEAD_DIM, ATT_HEAD_DIM), F32)).astype(BF16)
    return pl.pallas_call(
        kern,
        name=f"att_proj_d{dil}",
        grid=(B, S // ts),
        in_specs=x_specs + [
            _const_spec((1, D)),
            _const_spec((D, 3 * ATT_WIDTH)),
            _const_spec((1, ATT_HEAD_DIM)),
            _const_spec((1, ATT_HEAD_DIM)),
            _const_spec((ATT_PAIR_W, ATT_PAIR_W)),
            tab_spec, tab_spec,
        ],
        out_specs=[out_spec, out_spec, out_spec],
        out_shape=[out_sds, out_sds, out_sds],
        scratch_shapes=[pltpu.VMEM((ts, D), BF16)],
        compiler_params=pltpu.CompilerParams(
            dimension_semantics=("parallel", "parallel"), vmem_limit_bytes=VMEM_LIMIT_BYTES),
    )(*([x_view] * dil), norm_w.reshape(1, D), w_qkv, q_gain.reshape(1, -1), k_gain.reshape(1, -1),
      head_sum, *tables)


def _att_core_kernel(*refs, ts, seq):
    n_g = len(ATT_GROUPS)
    grp = [refs[7 * g:7 * g + 7] for g in range(n_g)]
    o_ref = refs[7 * n_g]
    og_s, lse_s = refs[7 * n_g + 1:]
    t = pl.program_id(1)
    side = ATT_SIDE
    hd = ATT_HEAD_DIM
    qb = ATT_QBLOCK
    nk = qb + 2 * side

    for g, (_, dil) in enumerate(ATT_GROUPS):
        q_ref, km_ref, kp_ref, kn_ref, vm_ref, vp_ref, vn_ref = grp[g]
        n = ts // dil
        n_sub = seq // dil

        def window(main_ref, prev_ref, next_ref, r, c0):
            parts = []
            if c0 == 0:
                parts.append(prev_ref[0, 0, r])
                lo = 0
            else:
                lo = c0 - side
            hi = min(c0 + qb + side, n)
            parts.append(main_ref[0, 0, r, lo:hi, :])
            if c0 + qb + side > n:
                parts.append(next_ref[0, 0, r])
            return parts[0] if len(parts) == 1 else jnp.concatenate(parts, axis=0)

        @pl.when(t >= 0)
        def _():
            a = lax.broadcasted_iota(jnp.int32, (qb, nk), 0)
            j = lax.broadcasted_iota(jnp.int32, (qb, nk), 1)
            band_bias = jnp.where((j - a >= 0) & (j - a <= 2 * side), 0.0, NEG_BIG).astype(F32)
            jrow = lax.broadcasted_iota(jnp.int32, (1, nk), 1)
            for r in range(dil):
                for c0 in range(0, n, qb):
                    base = t * n + c0 - side
                    key_ok = (jrow + base >= 0) & (jrow + base < n_sub)
                    bias = band_bias + jnp.where(key_ok, 0.0, NEG_BIG).astype(F32)
                    qv = q_ref[0, 0, r, c0:c0 + qb, :]
                    kv = window(km_ref, kp_ref, kn_ref, r, c0)
                    vv = window(vm_ref, vp_ref, vn_ref, r, c0)
                    for hh in range(2):
                        cols = slice(hh * hd, (hh + 1) * hd)
                        s = _dot_nt(qv[:, cols], kv[:, cols]) + bias
                        m = jnp.max(s, axis=-1, keepdims=True)
                        p = jnp.exp(s - m)
                        den = jnp.sum(p, axis=-1, keepdims=True)
                        o = _dot(p.astype(BF16), vv[:, cols]) / den
                        lse = jnp.broadcast_to(m + jnp.log(den), (qb, hd))
                        if dil == 1:
                            rows = slice(c0, c0 + qb)
                        else:
                            rows = pl.ds(c0 * dil + r, qb, stride=dil)
                        og_s[g, hh, rows, :] = o
                        lse_s[g, hh, rows, :] = lse

    for c0 in range(0, ts, ATT_MERGE_ROWS):
        rows = slice(c0, c0 + ATT_MERGE_ROWS)
        ys = []
        for hh in range(2):
            lses = [lse_s[g, hh, rows, :] for g in range(n_g)]
            m = functools.reduce(jnp.maximum, lses)
            ws = [jnp.exp(l - m) for l in lses]
            den = functools.reduce(lambda u, v: u + v, ws)
            num = functools.reduce(lambda u, v: u + v,
                                   [w * og_s[g, hh, rows, :] for g, w in enumerate(ws)])
            ys.append(num / den)
        o_ref[0, rows, :] = jnp.concatenate(ys, axis=1).astype(BF16)


def _att_core(qkv, S):
    B = qkv[0][0].shape[0]
    ts = ATT_TILE
    side = ATT_SIDE
    pw = ATT_PAIR_W
    in_specs, args = [], []
    for (_, dil), (q, k, v) in zip(ATT_GROUPS, qkv):
        n = ts // dil
        per = n // side
        last = S // dil // side - 1
        main = pl.BlockSpec((1, 1, dil, n, pw), lambda b, t, h: (b, h, 0, t, 0))
        prev = pl.BlockSpec((1, 1, dil, side, pw),
                            lambda b, t, h, per=per: (b, h, 0, jnp.maximum(t * per - 1, 0), 0))
        nxt = pl.BlockSpec((1, 1, dil, side, pw),
                           lambda b, t, h, per=per, last=last: (b, h, 0, jnp.minimum((t + 1) * per, last), 0))
        in_specs += [main, main, prev, nxt, main, prev, nxt]
        args += [q, k, k, k, v, v, v]
    kern = functools.partial(_att_core_kernel, ts=ts, seq=S)
    n_g = len(ATT_GROUPS)
    return pl.pallas_call(
        kern,
        name="att_core",
        grid=(B, S // ts, ATT_PAIRS),
        in_specs=in_specs,
        out_specs=pl.BlockSpec((1, ts, pw), lambda b, t, h: (b, t, h)),
        out_shape=jax.ShapeDtypeStruct((B, S, ATT_WIDTH), BF16),
        scratch_shapes=[
            pltpu.VMEM((n_g, 2, ts, ATT_HEAD_DIM), F32),
            pltpu.VMEM((n_g, 2, ts, ATT_HEAD_DIM), F32),
        ],
        compiler_params=pltpu.CompilerParams(
            dimension_semantics=("parallel", "parallel", "parallel"),
            vmem_limit_bytes=VMEM_LIMIT_BYTES),
    )(*args)


def _att_out_kernel(x_ref, a_ref, nw_ref, wgate_ref, wout_ref, o_ref):
    h = _rms(x_ref[0], nw_ref[...]).astype(BF16)
    gate = _dot(h, wgate_ref[...])
    y = (a_ref[0].astype(F32) * _silu(gate)).astype(BF16)
    o_ref[0] = x_ref[0] + _dot(y, wout_ref[...])


def _att_out(x, att, norm_w, w_gate, w_out):
    B, S, D = x.shape
    ts = ATT_OUT_TILE
    return pl.pallas_call(
        _att_out_kernel,
        name="att_out",
        grid=(B, S // ts),
        in_specs=[
            pl.BlockSpec((1, ts, D), lambda b, t: (b, t, 0)),
            pl.BlockSpec((1, ts, ATT_WIDTH), lambda b, t: (b, t, 0)),
            _const_spec((1, D)),
            _const_spec((D, ATT_WIDTH)),
            _const_spec((ATT_WIDTH, D)),
        ],
        out_specs=pl.BlockSpec((1, ts, D), lambda b, t: (b, t, 0)),
        out_shape=jax.ShapeDtypeStruct((B, S, D), F32),
        compiler_params=pltpu.CompilerParams(
            dimension_semantics=("parallel", "parallel"), vmem_limit_bytes=VMEM_LIMIT_BYTES),
    )(x, att, norm_w.reshape(1, D), w_gate, w_out)


def _att_layer(x, norm_w, w_in, q_norm, k_norm, w_out):
    B, S, D = x.shape
    n_g = len(ATT_GROUPS)
    w_in = w_in.astype(BF16)
    tables = _att_rope_tables(S)
    perm = _att_head_perm()
    qk_cols = (jnp.arange(2 * ATT_WIDTH).reshape(-1, ATT_HEAD_DIM)[:, perm]).reshape(-1)
    qkv = []
    for g, (_, dil) in enumerate(ATT_GROUPS):
        w_g = w_in[:, 3 * g * ATT_WIDTH:3 * (g + 1) * ATT_WIDTH]
        w_g = jnp.concatenate([w_g[:, qk_cols], w_g[:, 2 * ATT_WIDTH:]], axis=1)
        qkv.append(_att_proj(x, norm_w, w_g, q_norm[g][perm], k_norm[g][perm], tables, dil))
    att = _att_core(qkv, S)
    return _att_out(x, att, norm_w, w_in[:, 3 * n_g * ATT_WIDTH:], w_out.astype(BF16))


def _ret_rope_tables(S):
    half = RET_QK_DIM // 2
    inv_freq = 1.0 / (RET_THETA ** (jnp.arange(half, dtype=F32) * 2.0 / RET_QK_DIM))
    ang = jnp.arange(S, dtype=F32)[:, None] * inv_freq[None, :]
    return jnp.cos(ang), jnp.sin(ang)


def _ret_rotate(val, cos, sin, scale):
    half = RET_QK_DIM // 2
    x1, x2 = val[:, :half], val[:, half:]
    out = jnp.concatenate([x1 * cos - x2 * sin, x2 * cos + x1 * sin], axis=1)
    return out * scale if scale != 1.0 else out


def _ret_decays(lg, reverse):
    C = RET_CHUNK
    i = lax.broadcasted_iota(jnp.int32, (C, C), 0)
    jj = lax.broadcasted_iota(jnp.int32, (C, C), 1)
    col = lax.broadcasted_iota(jnp.int32, (C, 1), 0).astype(F32)
    if reverse:
        dist, keep = jj - i, jj > i
        q_dec = jnp.exp((C - col) * lg)
        k_dec = jnp.exp(col * lg)
    else:
        dist, keep = i - jj, i >= jj
        q_dec = jnp.exp((col + 1.0) * lg)
        k_dec = jnp.exp((C - 1.0 - col) * lg)
    decay = jnp.where(keep, jnp.exp(jnp.where(keep, dist, 0).astype(F32) * lg), 0.0)
    chunk_dec = jnp.exp(jnp.full((1, 1), float(C), F32) * lg)
    return decay, q_dec, k_dec, chunk_dec


def _ret_project_qk(h_s, w_ref, cos_ref, sin_ref, q_s, k_s):
    cos, sin = cos_ref[...], sin_ref[...]
    for hd in range(RET_HEADS):
        qc = slice(hd * RET_QK_DIM, (hd + 1) * RET_QK_DIM)
        kc = slice(RET_QK + hd * RET_QK_DIM, RET_QK + (hd + 1) * RET_QK_DIM)
        q_s[hd] = _ret_rotate(_dot(h_s[...], w_ref[:, qc]), cos, sin, 1.0)
        k_s[hd] = _ret_rotate(_dot(h_s[...], w_ref[:, kc]), cos, sin, RET_QK_DIM ** -0.5)


def _ret_scan_chunk(rows, lg_row, q_s, k_s, v_at, st_s, reverse, filler=None):
    heads = range(RET_HEADS)
    decs = [_ret_decays(lg_row(hd), reverse) for hd in heads]
    q = [q_s[hd, rows, :] for hd in heads]
    k = [k_s[hd, rows, :] for hd in heads]
    v = [v_at(hd) for hd in heads]
    inner = [_dot_nt(q[hd].astype(BF16), k[hd].astype(BF16)) for hd in heads]
    kv = [_dot_tn((k[hd] * decs[hd][2]).astype(BF16), v[hd]) for hd in heads]
    extra = filler() if filler is not None else None
    outs = []
    for hd in heads:
        decay, q_dec, _, chunk_dec = decs[hd]
        state = st_s[hd]
        outs.append(_dot((inner[hd] * decay).astype(BF16), v[hd])
                    + _dot((q[hd] * q_dec).astype(BF16), state.astype(BF16)))
        st_s[hd] = state * chunk_dec + kv[hd]
    return outs, extra


def _ret_fwd_kernel(lg_ref, x_ref, nw_ref, w_ref, cos_ref, sin_ref, f_ref, v_ref,
                    h_s, q_s, k_s, st_s, *, ts):
    @pl.when(pl.program_id(1) == 0)
    def _():
        st_s[...] = jnp.zeros_like(st_s)

    h_s[...] = _rms(x_ref[0], nw_ref[...]).astype(BF16)
    _ret_project_qk(h_s, w_ref, cos_ref, sin_ref, q_s, k_s)
    for hd in range(RET_HEADS):
        vc = slice(2 * RET_QK + hd * RET_V_DIM, 2 * RET_QK + (hd + 1) * RET_V_DIM)
        v_ref[0, :, hd * RET_V_DIM:(hd + 1) * RET_V_DIM] = _dot(h_s[...], w_ref[:, vc]).astype(BF16)
    for c in range(ts // RET_CHUNK):
        rows = slice(c * RET_CHUNK, (c + 1) * RET_CHUNK)
        outs, _ = _ret_scan_chunk(
            rows, lambda hd: lg_ref[0, hd], q_s, k_s,
            lambda hd: v_ref[0, rows, hd * RET_V_DIM:(hd + 1) * RET_V_DIM], st_s, False)
        for hd in range(RET_HEADS):
            f_ref[0, rows, hd * RET_V_DIM:(hd + 1) * RET_V_DIM] = outs[hd]


def _ret_bwd_kernel(lg_ref, x_ref, nw_ref, w_ref, cos_ref, sin_ref, f_ref, v_ref, wout_ref,
                    o_ref, h_s, q_s, k_s, y_s, st_s, *, ts):
    @pl.when(pl.program_id(1) == 0)
    def _():
        st_s[...] = jnp.zeros_like(st_s)

    h_s[...] = _rms(x_ref[0], nw_ref[...]).astype(BF16)
    _ret_project_qk(h_s, w_ref, cos_ref, sin_ref, q_s, k_s)
    for c in reversed(range(ts // RET_CHUNK)):
        rows = slice(c * RET_CHUNK, (c + 1) * RET_CHUNK)

        def gates():
            return [_dot(h_s[rows, :], w_ref[:, 2 * RET_QK + hd * RET_V_DIM:
                                             2 * RET_QK + (hd + 1) * RET_V_DIM])
                    for hd in range(RET_HEADS)]

        outs, gate = _ret_scan_chunk(
            rows, lambda hd: lg_ref[1, hd], q_s, k_s,
            lambda hd: v_ref[0, rows, hd * RET_V_DIM:(hd + 1) * RET_V_DIM], st_s, True, gates)
        for hd in range(RET_HEADS):
            oc = slice(hd * RET_V_DIM, (hd + 1) * RET_V_DIM)
            y = outs[hd] + f_ref[0, rows, oc]
            y = y * lax.rsqrt(jnp.mean(y * y, axis=-1, keepdims=True) + EPS)
            y_s[rows, oc] = (y * _silu(gate[hd])).astype(BF16)
    o_ref[0] = x_ref[0] + _dot(y_s[...], wout_ref[...])


def _ret_layer(x, norm_w, w_in, decay_exp, w_out):
    B, S, D = x.shape
    ts = RET_TILE
    nt = S // ts
    w_in = w_in.astype(BF16)
    w_fwd = jnp.concatenate([w_in[:, :2 * RET_QK], w_in[:, 4 * RET_QK:4 * RET_QK + RET_V]], axis=1)
    w_bwd = jnp.concatenate([w_in[:, 2 * RET_QK:4 * RET_QK], w_in[:, 4 * RET_QK + RET_V:]], axis=1)
    log_gamma = jnp.log1p(-jnp.exp2(-decay_exp.astype(F32)))
    cos, sin = _ret_rope_tables(S)
    half = RET_QK_DIM // 2
    smem = pl.BlockSpec(memory_space=pltpu.SMEM)
    cparams = pltpu.CompilerParams(
        dimension_semantics=("parallel", "arbitrary"), vmem_limit_bytes=VMEM_LIMIT_BYTES)
    state = pltpu.VMEM((RET_HEADS, RET_QK_DIM, RET_V_DIM), F32)
    qk_scratch = pltpu.VMEM((RET_HEADS, ts, RET_QK_DIM), F32)

    fwd, v = pl.pallas_call(
        functools.partial(_ret_fwd_kernel, ts=ts),
        name="ret_fwd",
        grid=(B, nt),
        in_specs=[
            smem,
            pl.BlockSpec((1, ts, D), lambda b, t: (b, t, 0)),
            _const_spec((1, D)),
            _const_spec((D, 2 * RET_QK + RET_V)),
            pl.BlockSpec((ts, half), lambda b, t: (t, 0)),
            pl.BlockSpec((ts, half), lambda b, t: (t, 0)),
        ],
        out_specs=[pl.BlockSpec((1, ts, RET_V), lambda b, t: (b, t, 0)),
                   pl.BlockSpec((1, ts, RET_V), lambda b, t: (b, t, 0))],
        out_shape=[jax.ShapeDtypeStruct((B, S, RET_V), F32),
                   jax.ShapeDtypeStruct((B, S, RET_V), BF16)],
        scratch_shapes=[pltpu.VMEM((ts, D), BF16), qk_scratch, qk_scratch, state],
        compiler_params=cparams,
    )(log_gamma, x, norm_w.reshape(1, D), w_fwd, cos, sin)

    rev = lambda b, t: (b, nt - 1 - t, 0)
    return pl.pallas_call(
        functools.partial(_ret_bwd_kernel, ts=ts),
        name="ret_bwd",
        grid=(B, nt),
        in_specs=[
            smem,
            pl.BlockSpec((1, ts, D), rev),
            _const_spec((1, D)),
            _const_spec((D, 2 * RET_QK + RET_V)),
            pl.BlockSpec((ts, half), lambda b, t: (nt - 1 - t, 0)),
            pl.BlockSpec((ts, half), lambda b, t: (nt - 1 - t, 0)),
            pl.BlockSpec((1, ts, RET_V), rev),
            pl.BlockSpec((1, ts, RET_V), rev),
            _const_spec((RET_V, D)),
        ],
        out_specs=pl.BlockSpec((1, ts, D), rev),
        out_shape=jax.ShapeDtypeStruct((B, S, D), F32),
        scratch_shapes=[pltpu.VMEM((ts, D), BF16), qk_scratch, qk_scratch,
                        pltpu.VMEM((ts, RET_V), BF16), state],
        compiler_params=cparams,
    )(log_gamma, x, norm_w.reshape(1, D), w_bwd, cos, sin, fwd, v, w_out.astype(BF16))


def kernel(x, pool_norm, pool_w_in, pool_w_group, pool_scale, pool_w_out,
           att_norm, att_w_in, att_q_norm, att_k_norm, att_w_out,
           ret_norm, ret_w_in, ret_decay, ret_w_out):
    depth = pool_norm.shape[0] + att_norm.shape[0] + ret_norm.shape[0]
    for layer in range(depth):
        kind, idx = layer % 3, layer // 3
        if kind == 0:
            x = _pool_layer(x, pool_norm[idx], pool_w_in[idx], pool_w_group[idx], pool_scale[idx],
                            pool_w_out[idx])
        elif kind == 1:
            x = _att_layer(x, att_norm[idx], att_w_in[idx], att_q_norm[idx], att_k_norm[idx],
                           att_w_out[idx])
        else:
            x = _ret_layer(x, ret_norm[idx], ret_w_in[idx], ret_decay[idx], ret_w_out[idx])
    return x
```

```python
import functools
import math

import jax
import jax.numpy as jnp
import numpy as np
from jax import lax
from jax.experimental import pallas as pl
from jax.experimental.pallas import tpu as pltpu

F32 = jnp.float32
BF16 = jnp.bfloat16

D_MODEL = 1024
EPS = 1e-6
NEG_BIG = -1e30

POOL_WINDOWS = (2, 4, 8, 16)
POOL_WIDTH = 2 * D_MODEL
POOL_GROUP = POOL_WIDTH // len(POOL_WINDOWS)

ATT_GROUPS = ((128, 1), (512, 4), (2048, 16))
ATT_HEAD_DIM = 128
ATT_HEADS = D_MODEL // ATT_HEAD_DIM
ATT_WIDTH = ATT_HEADS * ATT_HEAD_DIM
ATT_SIDE = 64
ROPE_DIM = ATT_HEAD_DIM // 4
ROPE_THETA = 500000.0

RET_HEADS = 4
RET_QK_DIM = D_MODEL // RET_HEADS
RET_V_DIM = 2 * D_MODEL // RET_HEADS
RET_QK = RET_HEADS * RET_QK_DIM
RET_V = RET_HEADS * RET_V_DIM
RET_CHUNK = 256
RET_THETA = 10000.0

V7X_VMEM_BYTES = 64 * 1024 * 1024
VMEM_LIMIT_BYTES = V7X_VMEM_BYTES - 8 * 1024 * 1024
LANES = 128
BF16_ROWS = 16

POOL_TILE = 512
POOL_HALO = BF16_ROWS
POOL_BLOCK = 128
ATT_PROJ_TILE = 1024
ATT_TILE = 2048
ATT_QBLOCK = 128
ATT_OUT_TILE = 1024
ATT_MERGE_ROWS = 32
ATT_PAIRS = ATT_HEADS // 2
ATT_PAIR_W = 2 * ATT_HEAD_DIM
RET_TILE = 512


def _const_spec(shape):
    nd = len(shape)
    return pl.BlockSpec(shape, lambda *_: (0,) * nd, pipeline_mode=pl.Buffered(1))


def _rms(v, w):
    return v * lax.rsqrt(jnp.mean(v * v, axis=-1, keepdims=True) + EPS) * w


def _silu(g):
    return g * (1.0 / (1.0 + jnp.exp(-g)))


def _dot(a, b):
    return jnp.dot(a, b, preferred_element_type=F32)


def _dot_nt(a, b):
    return lax.dot_general(a, b, (((1,), (1,)), ((), ())), preferred_element_type=F32)


def _dot_tn(a, b):
    return lax.dot_general(a, b, (((0,), (0,)), ((), ())), preferred_element_type=F32)


def _pool_kernel(x_ref, xp_ref, xn_ref, nw_ref, win_ref, wg_ref, sc_ref, wout_ref, band_ref,
                 o_ref, h_s, u_s, ub_s, d_s, y_s, *, ts, seq):
    t = pl.program_id(1)
    nt = pl.num_programs(1)
    nw = nw_ref[...]
    halo = POOL_HALO
    n_grp = len(POOL_WINDOWS)

    h_s[halo:halo + ts, :] = _rms(x_ref[0], nw).astype(BF16)
    hp = _rms(xp_ref[0], nw)
    h_s[0:halo, :] = jnp.where(t > 0, hp, 0.0).astype(BF16)
    hn = _rms(xn_ref[0], nw)
    h_s[halo + ts:, :] = jnp.where(t < nt - 1, hn, 0.0).astype(BF16)

    row = lax.broadcasted_iota(jnp.int32, (POOL_BLOCK, 1), 0)

    def project(g):
        ug = _dot(h_s[...], win_ref[:, g * POOL_GROUP:(g + 1) * POOL_GROUP])
        u_s[g] = ug
        ub_s[g] = ug.astype(BF16)

    def pool(g):
        window = POOL_WINDOWS[g]
        for b in range(ts // POOL_BLOCK):
            r0 = b * POOL_BLOCK
            wsum = _dot(band_ref[g], ub_s[g, r0:r0 + POOL_BLOCK + 2 * halo, :])
            pos = t * ts + r0 + row
            lo = jnp.maximum(pos - window // 2, 0)
            hi = jnp.minimum(pos - window // 2 + window, seq)
            diff = wsum / (hi - lo).astype(F32) - u_s[g, halo + r0:halo + r0 + POOL_BLOCK, :]
            d_s[g, r0:r0 + POOL_BLOCK, :] = diff.astype(BF16)

    def mix(g):
        cols = slice(g * POOL_GROUP, (g + 1) * POOL_GROUP)
        gate = _dot(h_s[halo:halo + ts, :], win_ref[:, POOL_WIDTH + g * POOL_GROUP:
                                                    POOL_WIDTH + (g + 1) * POOL_GROUP])
        mixed = _dot(d_s[g], wg_ref[g]) * sc_ref[:, cols]
        y_s[:, cols] = (mixed * _silu(gate)).astype(BF16)

    for step in range(n_grp + 2):
        if step < n_grp:
            project(step)
        if 0 <= step - 1 < n_grp:
            pool(step - 1)
        if 0 <= step - 2 < n_grp:
            mix(step - 2)
    o_ref[0] = x_ref[0] + _dot(y_s[...], wout_ref[...])


def _pool_band():
    i = jnp.arange(POOL_BLOCK)[:, None]
    j = jnp.arange(POOL_BLOCK + 2 * POOL_HALO)[None, :] - POOL_HALO
    bands = [((j - i >= -(w // 2)) & (j - i < w - w // 2)) for w in POOL_WINDOWS]
    return jnp.stack(bands).astype(BF16)


def _pool_layer(x, norm_w, w_in, w_group, scale, w_out):
    B, S, D = x.shape
    ts, halo = POOL_TILE, POOL_HALO
    nt = S // ts
    hb = ts // halo
    n_hb = S // halo
    kern = functools.partial(_pool_kernel, ts=ts, seq=S)
    return pl.pallas_call(
        kern,
        name="pool_layer",
        grid=(B, nt),
        in_specs=[
            pl.BlockSpec((1, ts, D), lambda b, t: (b, t, 0)),
            pl.BlockSpec((1, halo, D), lambda b, t: (b, jnp.maximum(t * hb - 1, 0), 0)),
            pl.BlockSpec((1, halo, D), lambda b, t: (b, jnp.minimum((t + 1) * hb, n_hb - 1), 0)),
            _const_spec((1, D)),
            _const_spec((D, 2 * POOL_WIDTH)),
            _const_spec((len(POOL_WINDOWS), POOL_GROUP, POOL_GROUP)),
            _const_spec((1, POOL_WIDTH)),
            _const_spec((POOL_WIDTH, D)),
            _const_spec((len(POOL_WINDOWS), POOL_BLOCK, POOL_BLOCK + 2 * halo)),
        ],
        out_specs=pl.BlockSpec((1, ts, D), lambda b, t: (b, t, 0)),
        out_shape=jax.ShapeDtypeStruct((B, S, D), F32),
        scratch_shapes=[
            pltpu.VMEM((ts + 2 * halo, D), BF16),
            pltpu.VMEM((len(POOL_WINDOWS), ts + 2 * halo, POOL_GROUP), F32),
            pltpu.VMEM((len(POOL_WINDOWS), ts + 2 * halo, POOL_GROUP), BF16),
            pltpu.VMEM((len(POOL_WINDOWS), ts, POOL_GROUP), BF16),
            pltpu.VMEM((ts, POOL_WIDTH), BF16),
        ],
        compiler_params=pltpu.CompilerParams(
            dimension_semantics=("parallel", "parallel"), vmem_limit_bytes=VMEM_LIMIT_BYTES),
    )(x, x, x, norm_w.reshape(1, D), w_in.astype(BF16), w_group.astype(BF16),
      scale.reshape(1, POOL_WIDTH), w_out.astype(BF16), _pool_band())


def _att_head_perm():
    half = ROPE_DIM // 2
    rest = (ATT_HEAD_DIM - ROPE_DIM) // 2
    order = (list(range(half)) + list(range(ROPE_DIM, ROPE_DIM + rest))
             + list(range(half, ROPE_DIM)) + list(range(ROPE_DIM + rest, ATT_HEAD_DIM)))
    return np.array(order, dtype=np.int32)


def _att_rope_tables(S):
    half = ROPE_DIM // 2
    inv_freq = 1.0 / (ROPE_THETA ** (jnp.arange(half, dtype=F32) * 2.0 / ROPE_DIM))
    ang = jnp.arange(S, dtype=F32)[:, None] * inv_freq[None, :]
    pad = ATT_HEAD_DIM // 2 - half
    cos, sin = jnp.cos(ang), jnp.sin(ang)
    one, zero = jnp.ones((S, pad), F32), jnp.zeros((S, pad), F32)
    return (jnp.concatenate([cos, one, cos, one], axis=1),
            jnp.concatenate([-sin, zero, sin, zero], axis=1))


def _att_proj_kernel(x_ref, nw_ref, w_ref, qn_ref, kn_ref, hsum_ref, cos_ref, sin_ref,
                     q_ref, k_ref, v_ref, h_s, hf_s, *, ts, dil):
    hd = ATT_HEAD_DIM
    n = ts // dil
    hn = _rms(x_ref[0], nw_ref[...])
    if dil == 1:
        h_s[...] = hn.astype(BF16)
    else:
        n_slab = hn.shape[1] // LANES
        for c in range(n_slab):
            hf_s[c] = hn[:, c * LANES:(c + 1) * LANES]
        for r in range(dil):
            rows = pl.ds(r, n, stride=dil)
            h_s[r * n:(r + 1) * n, :] = jnp.concatenate(
                [hf_s[c, rows, :] for c in range(n_slab)], axis=1).astype(BF16)
    cos_t, sin_t = cos_ref[...], sin_ref[...]

    def emit(val, out_ref, pair, hh):
        out_ref[0, pair, :, :, hh * hd:(hh + 1) * hd] = val.astype(BF16).reshape(dil, n, hd)

    def qk_head(val, inv_rms, gain):
        y = val * inv_rms * gain
        return y * cos_t + pltpu.roll(y, hd // 2, 1) * sin_t

    gains = (qn_ref[...] * hd ** -0.5, kn_ref[...])
    out_refs = (q_ref, k_ref, v_ref)
    n_pairs = 3 * ATT_PAIRS

    def project(i):
        return _dot(h_s[...], w_ref[:, i * ATT_PAIR_W:(i + 1) * ATT_PAIR_W])

    nxt = project(0)
    for i in range(n_pairs):
        proj = nxt
        if i + 1 < n_pairs:
            nxt = project(i + 1)
        kind, pair = divmod(i, ATT_PAIRS)
        if kind < 2:
            ssq = _dot((proj * proj).astype(BF16), hsum_ref[...])
            inv_rms = lax.rsqrt(ssq * (1.0 / hd) + EPS)
        for hh in range(2):
            lanes = slice(hh * hd, (hh + 1) * hd)
            if kind < 2:
                emit(qk_head(proj[:, lanes], inv_rms[:, lanes], gains[kind]), out_refs[kind], pair, hh)
            else:
                emit(proj[:, lanes], v_ref, pair, hh)


def _att_proj(x, norm_w, w_qkv, q_gain, k_gain, tables, dil):
    B, S, D = x.shape
    ts = ATT_PROJ_TILE
    n = ts // dil
    kern = functools.partial(_att_proj_kernel, ts=ts, dil=dil)
    tables = [tab.reshape(S // ts, n, dil, ATT_HEAD_DIM).transpose(0, 2, 1, 3).reshape(S, ATT_HEAD_DIM)
              for tab in tables]
    tab_spec = pl.BlockSpec((ts, ATT_HEAD_DIM), lambda b, t: (t, 0))
    out_spec = pl.BlockSpec((1, ATT_PAIRS, dil, n, ATT_PAIR_W), lambda b, t: (b, 0, 0, t, 0))
    out_sds = jax.ShapeDtypeStruct((B, ATT_PAIRS, dil, S // dil, ATT_PAIR_W), BF16)
    head_sum = jnp.kron(jnp.eye(2, dtype=F32), jnp.ones((ATT_HEAD_DIM, ATT_HEAD_DIM), F32)).astype(BF16)
    return pl.pallas_call(
        kern,
        name=f"att_proj_d{dil}",
        grid=(B, S // ts),
        in_specs=[
            pl.BlockSpec((1, ts, D), lambda b, t: (b, t, 0)),
            _const_spec((1, D)),
            _const_spec((D, 3 * ATT_WIDTH)),
            _const_spec((1, ATT_HEAD_DIM)),
            _const_spec((1, ATT_HEAD_DIM)),
            _const_spec((ATT_PAIR_W, ATT_PAIR_W)),
            tab_spec, tab_spec,
        ],
        out_specs=[out_spec, out_spec, out_spec],
        out_shape=[out_sds, out_sds, out_sds],
        scratch_shapes=[pltpu.VMEM((ts, D), BF16), pltpu.VMEM((D // LANES, ts, LANES), F32)],
        compiler_params=pltpu.CompilerParams(
            dimension_semantics=("parallel", "parallel"), vmem_limit_bytes=VMEM_LIMIT_BYTES),
    )(x, norm_w.reshape(1, D), w_qkv, q_gain.reshape(1, -1), k_gain.reshape(1, -1), head_sum,
      *tables)


def _att_core_kernel(*refs, ts, seq):
    n_g = len(ATT_GROUPS)
    grp = [refs[7 * g:7 * g + 7] for g in range(n_g)]
    o_ref = refs[7 * n_g]
    og_s, lse_s = refs[7 * n_g + 1:]
    t = pl.program_id(1)
    side = ATT_SIDE
    hd = ATT_HEAD_DIM
    qb = ATT_QBLOCK
    nk = qb + 2 * side

    for g, (_, dil) in enumerate(ATT_GROUPS):
        q_ref, km_ref, kp_ref, kn_ref, vm_ref, vp_ref, vn_ref = grp[g]
        n = ts // dil
        n_sub = seq // dil

        def window(main_ref, prev_ref, next_ref, r, c0):
            parts = []
            if c0 == 0:
                parts.append(prev_ref[0, 0, r])
                lo = 0
            else:
                lo = c0 - side
            hi = min(c0 + qb + side, n)
            parts.append(main_ref[0, 0, r, lo:hi, :])
            if c0 + qb + side > n:
                parts.append(next_ref[0, 0, r])
            return parts[0] if len(parts) == 1 else jnp.concatenate(parts, axis=0)

        @pl.when(t >= 0)
        def _():
            a = lax.broadcasted_iota(jnp.int32, (qb, nk), 0)
            j = lax.broadcasted_iota(jnp.int32, (qb, nk), 1)
            band_bias = jnp.where((j - a >= 0) & (j - a <= 2 * side), 0.0, NEG_BIG).astype(F32)
            jrow = lax.broadcasted_iota(jnp.int32, (1, nk), 1)
            for r in range(dil):
                for c0 in range(0, n, qb):
                    base = t * n + c0 - side
                    key_ok = (jrow + base >= 0) & (jrow + base < n_sub)
                    bias = band_bias + jnp.where(key_ok, 0.0, NEG_BIG).astype(F32)
                    qv = q_ref[0, 0, r, c0:c0 + qb, :]
                    kv = window(km_ref, kp_ref, kn_ref, r, c0)
                    vv = window(vm_ref, vp_ref, vn_ref, r, c0)
                    for hh in range(2):
                        cols = slice(hh * hd, (hh + 1) * hd)
                        s = _dot_nt(qv[:, cols], kv[:, cols]) + bias
                        m = jnp.max(s, axis=-1, keepdims=True)
                        p = jnp.exp(s - m)
                        den = jnp.sum(p, axis=-1, keepdims=True)
                        o = _dot(p.astype(BF16), vv[:, cols]) / den
                        lse = jnp.broadcast_to(m + jnp.log(den), (qb, hd))
                        if dil == 1:
                            rows = slice(c0, c0 + qb)
                        else:
                            rows = pl.ds(c0 * dil + r, qb, stride=dil)
                        og_s[g, hh, rows, :] = o
                        lse_s[g, hh, rows, :] = lse

    for c0 in range(0, ts, ATT_MERGE_ROWS):
        rows = slice(c0, c0 + ATT_MERGE_ROWS)
        ys = []
        for hh in range(2):
            lses = [lse_s[g, hh, rows, :] for g in range(n_g)]
            m = functools.reduce(jnp.maximum, lses)
            ws = [jnp.exp(l - m) for l in lses]
            den = functools.reduce(lambda u, v: u + v, ws)
            num = functools.reduce(lambda u, v: u + v,
                                   [w * og_s[g, hh, rows, :] for g, w in enumerate(ws)])
            ys.append(num / den)
        o_ref[0, rows, :] = jnp.concatenate(ys, axis=1).astype(BF16)


def _att_core(qkv, S):
    B = qkv[0][0].shape[0]
    ts = ATT_TILE
    side = ATT_SIDE
    pw = ATT_PAIR_W
    in_specs, args = [], []
    for (_, dil), (q, k, v) in zip(ATT_GROUPS, qkv):
        n = ts // dil
        per = n // side
        last = S // dil // side - 1
        main = pl.BlockSpec((1, 1, dil, n, pw), lambda b, t, h: (b, h, 0, t, 0))
        prev = pl.BlockSpec((1, 1, dil, side, pw),
                            lambda b, t, h, per=per: (b, h, 0, jnp.maximum(t * per - 1, 0), 0))
        nxt = pl.BlockSpec((1, 1, dil, side, pw),
                           lambda b, t, h, per=per, last=last: (b, h, 0, jnp.minimum((t + 1) * per, last), 0))
        in_specs += [main, main, prev, nxt, main, prev, nxt]
        args += [q, k, k, k, v, v, v]
    kern = functools.partial(_att_core_kernel, ts=ts, seq=S)
    n_g = len(ATT_GROUPS)
    return pl.pallas_call(
        kern,
        name="att_core",
        grid=(B, S // ts, ATT_PAIRS),
        in_specs=in_specs,
        out_specs=pl.BlockSpec((1, ts, pw), lambda b, t, h: (b, t, h)),
        out_shape=jax.ShapeDtypeStruct((B, S, ATT_WIDTH), BF16),
        scratch_shapes=[
            pltpu.VMEM((n_g, 2, ts, ATT_HEAD_DIM), F32),
            pltpu.VMEM((n_g, 2, ts, ATT_HEAD_DIM), F32),
        ],
        compiler_params=pltpu.CompilerParams(
            dimension_semantics=("parallel", "parallel", "parallel"),
            vmem_limit_bytes=VMEM_LIMIT_BYTES),
    )(*args)


def _att_out_kernel(x_ref, a_ref, nw_ref, wgate_ref, wout_ref, o_ref):
    h = _rms(x_ref[0], nw_ref[...]).astype(BF16)
    gate = _dot(h, wgate_ref[...])
    y = (a_ref[0].astype(F32) * _silu(gate)).astype(BF16)
    o_ref[0] = x_ref[0] + _dot(y, wout_ref[...])


def _att_out(x, att, norm_w, w_gate, w_out):
    B, S, D = x.shape
    ts = ATT_OUT_TILE
    return pl.pallas_call(
        _att_out_kernel,
        name="att_out",
        grid=(B, S // ts),
        in_specs=[
            pl.BlockSpec((1, ts, D), lambda b, t: (b, t, 0)),
            pl.BlockSpec((1, ts, ATT_WIDTH), lambda b, t: (b, t, 0)),
            _const_spec((1, D)),
            _const_spec((D, ATT_WIDTH)),
            _const_spec((ATT_WIDTH, D)),
        ],
        out_specs=pl.BlockSpec((1, ts, D), lambda b, t: (b, t, 0)),
        out_shape=jax.ShapeDtypeStruct((B, S, D), F32),
        compiler_params=pltpu.CompilerParams(
            dimension_semantics=("parallel", "parallel"), vmem_limit_bytes=VMEM_LIMIT_BYTES),
    )(x, att, norm_w.reshape(1, D), w_gate, w_out)


def _att_layer(x, norm_w, w_in, q_norm, k_norm, w_out):
    B, S, D = x.shape
    n_g = len(ATT_GROUPS)
    w_in = w_in.astype(BF16)
    tables = _att_rope_tables(S)
    perm = _att_head_perm()
    qk_cols = (jnp.arange(2 * ATT_WIDTH).reshape(-1, ATT_HEAD_DIM)[:, perm]).reshape(-1)
    qkv = []
    for g, (_, dil) in enumerate(ATT_GROUPS):
        w_g = w_in[:, 3 * g * ATT_WIDTH:3 * (g + 1) * ATT_WIDTH]
        w_g = jnp.concatenate([w_g[:, qk_cols], w_g[:, 2 * ATT_WIDTH:]], axis=1)
        qkv.append(_att_proj(x, norm_w, w_g, q_norm[g][perm], k_norm[g][perm], tables, dil))
    att = _att_core(qkv, S)
    return _att_out(x, att, norm_w, w_in[:, 3 * n_g * ATT_WIDTH:], w_out.astype(BF16))


def _ret_rope_tables(S):
    half = RET_QK_DIM // 2
    inv_freq = 1.0 / (RET_THETA ** (jnp.arange(half, dtype=F32) * 2.0 / RET_QK_DIM))
    ang = jnp.arange(S, dtype=F32)[:, None] * inv_freq[None, :]
    return jnp.cos(ang), jnp.sin(ang)


def _ret_rotate(val, cos, sin, scale):
    half = RET_QK_DIM // 2
    x1, x2 = val[:, :half], val[:, half:]
    out = jnp.concatenate([x1 * cos - x2 * sin, x2 * cos + x1 * sin], axis=1)
    return out * scale if scale != 1.0 else out


def _ret_decays(lg, reverse):
    C = RET_CHUNK
    i = lax.broadcasted_iota(jnp.int32, (C, C), 0)
    jj = lax.broadcasted_iota(jnp.int32, (C, C), 1)
    col = lax.broadcasted_iota(jnp.int32, (C, 1), 0).astype(F32)
    if reverse:
        dist, keep = jj - i, jj > i
        q_dec = jnp.exp((C - col) * lg)
        k_dec = jnp.exp(col * lg)
    else:
        dist, keep = i - jj, i >= jj
        q_dec = jnp.exp((col + 1.0) * lg)
        k_dec = jnp.exp((C - 1.0 - col) * lg)
    decay = jnp.where(keep, jnp.exp(jnp.where(keep, dist, 0).astype(F32) * lg), 0.0)
    chunk_dec = jnp.exp(jnp.full((1, 1), float(C), F32) * lg)
    return decay, q_dec, k_dec, chunk_dec


def _ret_project_qk(h_s, w_ref, cos_ref, sin_ref, q_s, k_s):
    cos, sin = cos_ref[...], sin_ref[...]
    for hd in range(RET_HEADS):
        qc = slice(hd * RET_QK_DIM, (hd + 1) * RET_QK_DIM)
        kc = slice(RET_QK + hd * RET_QK_DIM, RET_QK + (hd + 1) * RET_QK_DIM)
        q_s[hd] = _ret_rotate(_dot(h_s[...], w_ref[:, qc]), cos, sin, 1.0)
        k_s[hd] = _ret_rotate(_dot(h_s[...], w_ref[:, kc]), cos, sin, RET_QK_DIM ** -0.5)


def _ret_scan_chunk(rows, lg_row, q_s, k_s, v_at, st_s, reverse, filler=None):
    heads = range(RET_HEADS)
    decs = [_ret_decays(lg_row(hd), reverse) for hd in heads]
    q = [q_s[hd, rows, :] for hd in heads]
    k = [k_s[hd, rows, :] for hd in heads]
    v = [v_at(hd) for hd in heads]
    inner = [_dot_nt(q[hd].astype(BF16), k[hd].astype(BF16)) for hd in heads]
    kv = [_dot_tn((k[hd] * decs[hd][2]).astype(BF16), v[hd]) for hd in heads]
    extra = filler() if filler is not None else None
    outs = []
    for hd in heads:
        decay, q_dec, _, chunk_dec = decs[hd]
        state = st_s[hd]
        outs.append(_dot((inner[hd] * decay).astype(BF16), v[hd])
                    + _dot((q[hd] * q_dec).astype(BF16), state.astype(BF16)))
        st_s[hd] = state * chunk_dec + kv[hd]
    return outs, extra


def _ret_fwd_kernel(lg_ref, x_ref, nw_ref, w_ref, cos_ref, sin_ref, f_ref, v_ref,
                    h_s, q_s, k_s, st_s, *, ts):
    @pl.when(pl.program_id(1) == 0)
    def _():
        st_s[...] = jnp.zeros_like(st_s)

    h_s[...] = _rms(x_ref[0], nw_ref[...]).astype(BF16)
    _ret_project_qk(h_s, w_ref, cos_ref, sin_ref, q_s, k_s)
    for hd in range(RET_HEADS):
        vc = slice(2 * RET_QK + hd * RET_V_DIM, 2 * RET_QK + (hd + 1) * RET_V_DIM)
        v_ref[0, :, hd * RET_V_DIM:(hd + 1) * RET_V_DIM] = _dot(h_s[...], w_ref[:, vc]).astype(BF16)
    for c in range(ts // RET_CHUNK):
        rows = slice(c * RET_CHUNK, (c + 1) * RET_CHUNK)
        outs, _ = _ret_scan_chunk(
            rows, lambda hd: lg_ref[0, hd], q_s, k_s,
            lambda hd: v_ref[0, rows, hd * RET_V_DIM:(hd + 1) * RET_V_DIM], st_s, False)
        for hd in range(RET_HEADS):
            f_ref[0, rows, hd * RET_V_DIM:(hd + 1) * RET_V_DIM] = outs[hd]


def _ret_bwd_kernel(lg_ref, x_ref, nw_ref, w_ref, cos_ref, sin_ref, f_ref, v_ref, wout_ref,
                    o_ref, h_s, q_s, k_s, y_s, st_s, *, ts):
    @pl.when(pl.program_id(1) == 0)
    def _():
        st_s[...] = jnp.zeros_like(st_s)

    h_s[...] = _rms(x_ref[0], nw_ref[...]).astype(BF16)
    _ret_project_qk(h_s, w_ref, cos_ref, sin_ref, q_s, k_s)
    for c in reversed(range(ts // RET_CHUNK)):
        rows = slice(c * RET_CHUNK, (c + 1) * RET_CHUNK)

        def gates():
            return [_dot(h_s[rows, :], w_ref[:, 2 * RET_QK + hd * RET_V_DIM:
                                             2 * RET_QK + (hd + 1) * RET_V_DIM])
                    for hd in range(RET_HEADS)]

        outs, gate = _ret_scan_chunk(
            rows, lambda hd: lg_ref[1, hd], q_s, k_s,
            lambda hd: v_ref[0, rows, hd * RET_V_DIM:(hd + 1) * RET_V_DIM], st_s, True, gates)
        for hd in range(RET_HEADS):
            oc = slice(hd * RET_V_DIM, (hd + 1) * RET_V_DIM)
            y = outs[hd] + f_ref[0, rows, oc]
            y = y * lax.rsqrt(jnp.mean(y * y, axis=-1, keepdims=True) + EPS)
            y_s[rows, oc] = (y * _silu(gate[hd])).astype(BF16)
    o_ref[0] = x_ref[0] + _dot(y_s[...], wout_ref[...])


def _ret_layer(x, norm_w, w_in, decay_exp, w_out):
    B, S, D = x.shape
    ts = RET_TILE
    nt = S // ts
    w_in = w_in.astype(BF16)
    w_fwd = jnp.concatenate([w_in[:, :2 * RET_QK], w_in[:, 4 * RET_QK:4 * RET_QK + RET_V]], axis=1)
    w_bwd = jnp.concatenate([w_in[:, 2 * RET_QK:4 * RET_QK], w_in[:, 4 * RET_QK + RET_V:]], axis=1)
    log_gamma = jnp.log1p(-jnp.exp2(-decay_exp.astype(F32)))
    cos, sin = _ret_rope_tables(S)
    half = RET_QK_DIM // 2
    smem = pl.BlockSpec(memory_space=pltpu.SMEM)
    cparams = pltpu.CompilerParams(
        dimension_semantics=("parallel", "arbitrary"), vmem_limit_bytes=VMEM_LIMIT_BYTES)
    state = pltpu.VMEM((RET_HEADS, RET_QK_DIM, RET_V_DIM), F32)
    qk_scratch = pltpu.VMEM((RET_HEADS, ts, RET_QK_DIM), F32)

    fwd, v = pl.pallas_call(
        functools.partial(_ret_fwd_kernel, ts=ts),
        name="ret_fwd",
        grid=(B, nt),
        in_specs=[
            smem,
            pl.BlockSpec((1, ts, D), lambda b, t: (b, t, 0)),
            _const_spec((1, D)),
            _const_spec((D, 2 * RET_QK + RET_V)),
            pl.BlockSpec((ts, half), lambda b, t: (t, 0)),
            pl.BlockSpec((ts, half), lambda b, t: (t, 0)),
        ],
        out_specs=[pl.BlockSpec((1, ts, RET_V), lambda b, t: (b, t, 0)),
                   pl.BlockSpec((1, ts, RET_V), lambda b, t: (b, t, 0))],
        out_shape=[jax.ShapeDtypeStruct((B, S, RET_V), F32),
                   jax.ShapeDtypeStruct((B, S, RET_V), BF16)],
        scratch_shapes=[pltpu.VMEM((ts, D), BF16), qk_scratch, qk_scratch, state],
        compiler_params=cparams,
    )(log_gamma, x, norm_w.reshape(1, D), w_fwd, cos, sin)

    rev = lambda b, t: (b, nt - 1 - t, 0)
    return pl.pallas_call(
        functools.partial(_ret_bwd_kernel, ts=ts),
        name="ret_bwd",
        grid=(B, nt),
        in_specs=[
            smem,
            pl.BlockSpec((1, ts, D), rev),
            _const_spec((1, D)),
            _const_spec((D, 2 * RET_QK + RET_V)),
            pl.BlockSpec((ts, half), lambda b, t: (nt - 1 - t, 0)),
            pl.BlockSpec((ts, half), lambda b, t: (nt - 1 - t, 0)),
            pl.BlockSpec((1, ts, RET_V), rev),
            pl.BlockSpec((1, ts, RET_V), rev),
            _const_spec((RET_V, D)),
        ],
        out_specs=pl.BlockSpec((1, ts, D), rev),
        out_shape=jax.ShapeDtypeStruct((B, S, D), F32),
        scratch_shapes=[pltpu.VMEM((ts, D), BF16), qk_scratch, qk_scratch,
                        pltpu.VMEM((ts, RET_V), BF16), state],
        compiler_params=cparams,
    )(log_gamma, x, norm_w.reshape(1, D), w_bwd, cos, sin, fwd, v, w_out.astype(BF16))


def kernel(x, pool_norm, pool_w_in, pool_w_group, pool_scale, pool_w_out,
           att_norm, att_w_in, att_q_norm, att_k_norm, att_w_out,
           ret_norm, ret_w_in, ret_decay, ret_w_out):
    depth = pool_norm.shape[0] + att_norm.shape[0] + ret_norm.shape[0]
    for layer in range(depth):
        kind, idx = layer % 3, layer // 3
        if kind == 0:
            x = _pool_layer(x, pool_norm[idx], pool_w_in[idx], pool_w_group[idx], pool_scale[idx],
                            pool_w_out[idx])
        elif kind == 1:
            x = _att_layer(x, att_norm[idx], att_w_in[idx], att_q_norm[idx], att_k_norm[idx],
                           att_w_out[idx])
        else:
            x = _ret_layer(x, ret_norm[idx], ret_w_in[idx], ret_decay[idx], ret_w_out[idx])
    return x
```

```python
import functools
import math

import jax
import jax.numpy as jnp
import numpy as np
from jax import lax
from jax.experimental import pallas as pl
from jax.experimental.pallas import tpu as pltpu

F32 = jnp.float32
BF16 = jnp.bfloat16

D_MODEL = 1024
EPS = 1e-6
NEG_BIG = -1e30
LOG2E = math.log2(math.e)
LN2 = math.log(2.0)

POOL_WINDOWS = (2, 4, 8, 16)
POOL_WIDTH = 2 * D_MODEL
POOL_GROUP = POOL_WIDTH // len(POOL_WINDOWS)

ATT_GROUPS = ((128, 1), (512, 4), (2048, 16))
ATT_HEAD_DIM = 128
ATT_HEADS = D_MODEL // ATT_HEAD_DIM
ATT_WIDTH = ATT_HEADS * ATT_HEAD_DIM
ATT_SIDE = 64
ROPE_DIM = ATT_HEAD_DIM // 4
ROPE_THETA = 500000.0

RET_HEADS = 4
RET_QK_DIM = D_MODEL // RET_HEADS
RET_V_DIM = 2 * D_MODEL // RET_HEADS
RET_QK = RET_HEADS * RET_QK_DIM
RET_V = RET_HEADS * RET_V_DIM
RET_CHUNK = 256
RET_THETA = 10000.0

V7X_VMEM_BYTES = 64 * 1024 * 1024
VMEM_LIMIT_BYTES = V7X_VMEM_BYTES - 8 * 1024 * 1024
LANES = 128
BF16_ROWS = 16

POOL_TILE = 512
POOL_HALO = BF16_ROWS
POOL_BLOCK = 128
ATT_PROJ_TILE = 1024
ATT_TILE = 2048
ATT_QBLOCK = 128
ATT_OUT_TILE = 1024
ATT_MERGE_ROWS = 32
ATT_PAIRS = ATT_HEADS // 2
ATT_PAIR_W = 2 * ATT_HEAD_DIM
RET_TILE = 512


def _const_spec(shape):
    nd = len(shape)
    return pl.BlockSpec(shape, lambda *_: (0,) * nd, pipeline_mode=pl.Buffered(1))


def _rms(v, w):
    return v * lax.rsqrt(jnp.mean(v * v, axis=-1, keepdims=True) + EPS) * w


def _silu(g):
    return g * (1.0 / (1.0 + jnp.exp(-g)))


def _rope_angles(S, half, rot_dim, theta):
    inv_freq = (np.float32(1.0) / np.power(np.float32(theta),
                                           np.arange(half, dtype=np.float32) * np.float32(2.0 / rot_dim)))
    ang = np.arange(S, dtype=np.float32)[:, None] * inv_freq[None, :]
    return np.cos(ang).astype(np.float32), np.sin(ang).astype(np.float32)


def _dot(a, b):
    return jnp.dot(a, b, preferred_element_type=F32)


def _dot_nt(a, b):
    return lax.dot_general(a, b, (((1,), (1,)), ((), ())), preferred_element_type=F32)


def _dot_tn(a, b):
    return lax.dot_general(a, b, (((0,), (0,)), ((), ())), preferred_element_type=F32)


def _pool_kernel(x_ref, xp_ref, xn_ref, nw_ref, win_ref, wg_ref, sc_ref, wout_ref, band_ref,
                 o_ref, h_s, u_s, ub_s, d_s, y_s, *, ts, seq):
    t = pl.program_id(1)
    nt = pl.num_programs(1)
    nw = nw_ref[...]
    halo = POOL_HALO
    n_grp = len(POOL_WINDOWS)

    h_s[halo:halo + ts, :] = _rms(x_ref[0], nw).astype(BF16)
    hp = _rms(xp_ref[0], nw)
    h_s[0:halo, :] = jnp.where(t > 0, hp, 0.0).astype(BF16)
    hn = _rms(xn_ref[0], nw)
    h_s[halo + ts:, :] = jnp.where(t < nt - 1, hn, 0.0).astype(BF16)

    row = lax.broadcasted_iota(jnp.int32, (POOL_BLOCK, 1), 0)

    def project(g):
        ug = _dot(h_s[...], win_ref[:, g * POOL_GROUP:(g + 1) * POOL_GROUP])
        u_s[g] = ug
        ub_s[g] = ug.astype(BF16)

    def pool(g):
        window = POOL_WINDOWS[g]
        for b in range(ts // POOL_BLOCK):
            r0 = b * POOL_BLOCK
            wsum = _dot(band_ref[g], ub_s[g, r0:r0 + POOL_BLOCK + 2 * halo, :])
            pos = t * ts + r0 + row
            lo = jnp.maximum(pos - window // 2, 0)
            hi = jnp.minimum(pos - window // 2 + window, seq)
            diff = wsum / (hi - lo).astype(F32) - u_s[g, halo + r0:halo + r0 + POOL_BLOCK, :]
            d_s[g, r0:r0 + POOL_BLOCK, :] = diff.astype(BF16)

    def mix(g):
        cols = slice(g * POOL_GROUP, (g + 1) * POOL_GROUP)
        gate = _dot(h_s[halo:halo + ts, :], win_ref[:, POOL_WIDTH + g * POOL_GROUP:
                                                    POOL_WIDTH + (g + 1) * POOL_GROUP])
        mixed = _dot(d_s[g], wg_ref[g]) * sc_ref[:, cols]
        y_s[:, cols] = (mixed * _silu(gate)).astype(BF16)

    for step in range(n_grp + 2):
        if step < n_grp:
            project(step)
        if 0 <= step - 1 < n_grp:
            pool(step - 1)
        if 0 <= step - 2 < n_grp:
            mix(step - 2)
    o_ref[0] = x_ref[0] + _dot(y_s[...], wout_ref[...])


def _pool_band():
    i = np.arange(POOL_BLOCK)[:, None]
    j = np.arange(POOL_BLOCK + 2 * POOL_HALO)[None, :] - POOL_HALO
    bands = [((j - i >= -(w // 2)) & (j - i < w - w // 2)) for w in POOL_WINDOWS]
    return jnp.asarray(np.stack(bands).astype(np.float32), dtype=BF16)


def _pool_layer(x, norm_w, w_in, w_group, scale, w_out):
    B, S, D = x.shape
    ts, halo = POOL_TILE, POOL_HALO
    nt = S // ts
    hb = ts // halo
    n_hb = S // halo
    kern = functools.partial(_pool_kernel, ts=ts, seq=S)
    return pl.pallas_call(
        kern,
        name="pool_layer",
        grid=(B, nt),
        in_specs=[
            pl.BlockSpec((1, ts, D), lambda b, t: (b, t, 0)),
            pl.BlockSpec((1, halo, D), lambda b, t: (b, jnp.maximum(t * hb - 1, 0), 0)),
            pl.BlockSpec((1, halo, D), lambda b, t: (b, jnp.minimum((t + 1) * hb, n_hb - 1), 0)),
            _const_spec((1, D)),
            _const_spec((D, 2 * POOL_WIDTH)),
            _const_spec((len(POOL_WINDOWS), POOL_GROUP, POOL_GROUP)),
            _const_spec((1, POOL_WIDTH)),
            _const_spec((POOL_WIDTH, D)),
            _const_spec((len(POOL_WINDOWS), POOL_BLOCK, POOL_BLOCK + 2 * halo)),
        ],
        out_specs=pl.BlockSpec((1, ts, D), lambda b, t: (b, t, 0)),
        out_shape=jax.ShapeDtypeStruct((B, S, D), F32),
        scratch_shapes=[
            pltpu.VMEM((ts + 2 * halo, D), BF16),
            pltpu.VMEM((len(POOL_WINDOWS), ts + 2 * halo, POOL_GROUP), F32),
            pltpu.VMEM((len(POOL_WINDOWS), ts + 2 * halo, POOL_GROUP), BF16),
            pltpu.VMEM((len(POOL_WINDOWS), ts, POOL_GROUP), BF16),
            pltpu.VMEM((ts, POOL_WIDTH), BF16),
        ],
        compiler_params=pltpu.CompilerParams(
            dimension_semantics=("parallel", "parallel"), vmem_limit_bytes=VMEM_LIMIT_BYTES),
    )(x, x, x, norm_w.reshape(1, D), w_in.astype(BF16), w_group.astype(BF16),
      scale.reshape(1, POOL_WIDTH), w_out.astype(BF16), _pool_band())


def _att_head_perm():
    half = ROPE_DIM // 2
    rest = (ATT_HEAD_DIM - ROPE_DIM) // 2
    order = (list(range(half)) + list(range(ROPE_DIM, ROPE_DIM + rest))
             + list(range(half, ROPE_DIM)) + list(range(ROPE_DIM + rest, ATT_HEAD_DIM)))
    return np.array(order, dtype=np.int32)


def _att_rope_tables(S):
    half = ROPE_DIM // 2
    cos, sin = _rope_angles(S, half, ROPE_DIM, ROPE_THETA)
    pad = ATT_HEAD_DIM // 2 - half
    one, zero = np.ones((S, pad), np.float32), np.zeros((S, pad), np.float32)
    return (np.concatenate([cos, one, cos, one], axis=1),
            np.concatenate([-sin, zero, sin, zero], axis=1))


def _att_proj_kernel(x_ref, nw_ref, w_ref, qn_ref, kn_ref, hsum_ref, cos_ref, sin_ref,
                     q_ref, k_ref, v_ref, h_s, hf_s, *, ts, dil):
    hd = ATT_HEAD_DIM
    n = ts // dil
    hn = _rms(x_ref[0], nw_ref[...])
    if dil == 1:
        h_s[...] = hn.astype(BF16)
    else:
        n_slab = hn.shape[1] // LANES
        for c in range(n_slab):
            hf_s[c] = hn[:, c * LANES:(c + 1) * LANES]
        for r in range(dil):
            rows = pl.ds(r, n, stride=dil)
            h_s[r * n:(r + 1) * n, :] = jnp.concatenate(
                [hf_s[c, rows, :] for c in range(n_slab)], axis=1).astype(BF16)
    cos_t, sin_t = cos_ref[...], sin_ref[...]

    def emit(val, out_ref, pair, hh):
        out_ref[0, pair, :, :, hh * hd:(hh + 1) * hd] = val.astype(BF16).reshape(dil, n, hd)

    def qk_head(val, inv_rms, gain):
        y = val * inv_rms * gain
        return y * cos_t + pltpu.roll(y, hd // 2, 1) * sin_t

    gains = (qn_ref[...] * (hd ** -0.5 * LOG2E), kn_ref[...])
    out_refs = (q_ref, k_ref, v_ref)
    n_pairs = 3 * ATT_PAIRS

    def project(i):
        return _dot(h_s[...], w_ref[:, i * ATT_PAIR_W:(i + 1) * ATT_PAIR_W])

    nxt = project(0)
    for i in range(n_pairs):
        proj = nxt
        if i + 1 < n_pairs:
            nxt = project(i + 1)
        kind, pair = divmod(i, ATT_PAIRS)
        if kind < 2:
            ssq = _dot((proj * proj).astype(BF16), hsum_ref[...])
            inv_rms = lax.rsqrt(ssq * (1.0 / hd) + EPS)
        for hh in range(2):
            lanes = slice(hh * hd, (hh + 1) * hd)
            if kind < 2:
                emit(qk_head(proj[:, lanes], inv_rms[:, lanes], gains[kind]), out_refs[kind], pair, hh)
            else:
                emit(proj[:, lanes], v_ref, pair, hh)


def _att_proj(x, norm_w, w_qkv, q_gain, k_gain, tables, dil):
    B, S, D = x.shape
    ts = ATT_PROJ_TILE
    n = ts // dil
    kern = functools.partial(_att_proj_kernel, ts=ts, dil=dil)
    tables = [tab.reshape(S // ts, n, dil, ATT_HEAD_DIM).transpose(0, 2, 1, 3).reshape(S, ATT_HEAD_DIM)
              for tab in tables]
    tab_spec = pl.BlockSpec((ts, ATT_HEAD_DIM), lambda b, t: (t, 0))
    out_spec = pl.BlockSpec((1, ATT_PAIRS, dil, n, ATT_PAIR_W), lambda b, t: (b, 0, 0, t, 0))
    out_sds = jax.ShapeDtypeStruct((B, ATT_PAIRS, dil, S // dil, ATT_PAIR_W), BF16)
    head_sum = jnp.asarray(np.kron(np.eye(2, dtype=np.float32),
                                   np.ones((ATT_HEAD_DIM, ATT_HEAD_DIM), np.float32)), dtype=BF16)
    return pl.pallas_call(
        kern,
        name=f"att_proj_d{dil}",
        grid=(B, S // ts),
        in_specs=[
            pl.BlockSpec((1, ts, D), lambda b, t: (b, t, 0)),
            _const_spec((1, D)),
            _const_spec((D, 3 * ATT_WIDTH)),
            _const_spec((1, ATT_HEAD_DIM)),
            _const_spec((1, ATT_HEAD_DIM)),
            _const_spec((ATT_PAIR_W, ATT_PAIR_W)),
            tab_spec, tab_spec,
        ],
        out_specs=[out_spec, out_spec, out_spec],
        out_shape=[out_sds, out_sds, out_sds],
        scratch_shapes=[pltpu.VMEM((ts, D), BF16), pltpu.VMEM((D // LANES, ts, LANES), F32)],
        compiler_params=pltpu.CompilerParams(
            dimension_semantics=("parallel", "parallel"), vmem_limit_bytes=VMEM_LIMIT_BYTES),
    )(x, norm_w.reshape(1, D), w_qkv, q_gain.reshape(1, -1), k_gain.reshape(1, -1), head_sum,
      *tables)


def _att_core_kernel(*refs, ts, seq):
    n_g = len(ATT_GROUPS)
    grp = [refs[7 * g:7 * g + 7] for g in range(n_g)]
    o_ref = refs[7 * n_g]
    og_s, lse_s = refs[7 * n_g + 1:]
    t = pl.program_id(1)
    side = ATT_SIDE
    hd = ATT_HEAD_DIM
    qb = ATT_QBLOCK
    nk = qb + 2 * side

    for g, (_, dil) in enumerate(ATT_GROUPS):
        q_ref, km_ref, kp_ref, kn_ref, vm_ref, vp_ref, vn_ref = grp[g]
        n = ts // dil
        n_sub = seq // dil

        def window(main_ref, prev_ref, next_ref, r, c0):
            parts = []
            if c0 == 0:
                parts.append(prev_ref[0, 0, r])
                lo = 0
            else:
                lo = c0 - side
            hi = min(c0 + qb + side, n)
            parts.append(main_ref[0, 0, r, lo:hi, :])
            if c0 + qb + side > n:
                parts.append(next_ref[0, 0, r])
            return parts[0] if len(parts) == 1 else jnp.concatenate(parts, axis=0)

        @pl.when(t >= 0)
        def _():
            a = lax.broadcasted_iota(jnp.int32, (qb, nk), 0)
            j = lax.broadcasted_iota(jnp.int32, (qb, nk), 1)
            band_bias = jnp.where((j - a >= 0) & (j - a <= 2 * side), 0.0, NEG_BIG).astype(F32)
            jrow = lax.broadcasted_iota(jnp.int32, (1, nk), 1)
            ones = jnp.ones((nk, hd), BF16)
            for r in range(dil):
                for c0 in range(0, n, qb):
                    base = t * n + c0 - side
                    key_ok = (jrow + base >= 0) & (jrow + base < n_sub)
                    bias = band_bias + jnp.where(key_ok, 0.0, NEG_BIG).astype(F32)
                    qv = q_ref[0, 0, r, c0:c0 + qb, :]
                    kv = window(km_ref, kp_ref, kn_ref, r, c0)
                    vv = window(vm_ref, vp_ref, vn_ref, r, c0)
                    for hh in range(2):
                        cols = slice(hh * hd, (hh + 1) * hd)
                        s = _dot_nt(qv[:, cols], kv[:, cols]) + bias
                        m = jnp.max(s, axis=-1, keepdims=True)
                        p = jnp.exp2(s - m).astype(BF16)
                        acc = _dot(p, jnp.concatenate([vv[:, cols], ones], axis=1))
                        den = acc[:, hd:]
                        o = acc[:, :hd] / den
                        lse = m * LN2 + jnp.log(den)
                        if dil == 1:
                            rows = slice(c0, c0 + qb)
                        else:
                            rows = pl.ds(c0 * dil + r, qb, stride=dil)
                        og_s[g, hh, rows, :] = o
                        lse_s[g, hh, rows, :] = lse

    for c0 in range(0, ts, ATT_MERGE_ROWS):
        rows = slice(c0, c0 + ATT_MERGE_ROWS)
        ys = []
        for hh in range(2):
            lses = [lse_s[g, hh, rows, :] for g in range(n_g)]
            m = functools.reduce(jnp.maximum, lses)
            ws = [jnp.exp(l - m) for l in lses]
            den = functools.reduce(lambda u, v: u + v, ws)
            num = functools.reduce(lambda u, v: u + v,
                                   [w * og_s[g, hh, rows, :] for g, w in enumerate(ws)])
            ys.append(num / den)
        o_ref[0, rows, :] = jnp.concatenate(ys, axis=1).astype(BF16)


def _att_core(qkv, S):
    B = qkv[0][0].shape[0]
    ts = ATT_TILE
    side = ATT_SIDE
    pw = ATT_PAIR_W
    in_specs, args = [], []
    for (_, dil), (q, k, v) in zip(ATT_GROUPS, qkv):
        n = ts // dil
        per = n // side
        last = S // dil // side - 1
        main = pl.BlockSpec((1, 1, dil, n, pw), lambda b, t, h: (b, h, 0, t, 0))
        prev = pl.BlockSpec((1, 1, dil, side, pw),
                            lambda b, t, h, per=per: (b, h, 0, jnp.maximum(t * per - 1, 0), 0))
        nxt = pl.BlockSpec((1, 1, dil, side, pw),
                           lambda b, t, h, per=per, last=last: (b, h, 0, jnp.minimum((t + 1) * per, last), 0))
        in_specs += [main, main, prev, nxt, main, prev, nxt]
        args += [q, k, k, k, v, v, v]
    kern = functools.partial(_att_core_kernel, ts=ts, seq=S)
    n_g = len(ATT_GROUPS)
    return pl.pallas_call(
        kern,
        name="att_core",
        grid=(B, S // ts, ATT_PAIRS),
        in_specs=in_specs,
        out_specs=pl.BlockSpec((1, ts, pw), lambda b, t, h: (b, t, h)),
        out_shape=jax.ShapeDtypeStruct((B, S, ATT_WIDTH), BF16),
        scratch_shapes=[
            pltpu.VMEM((n_g, 2, ts, ATT_HEAD_DIM), F32),
            pltpu.VMEM((n_g, 2, ts, ATT_HEAD_DIM), F32),
        ],
        compiler_params=pltpu.CompilerParams(
            dimension_semantics=("parallel", "parallel", "parallel"),
            vmem_limit_bytes=VMEM_LIMIT_BYTES),
    )(*args)


def _att_out_kernel(x_ref, a_ref, nw_ref, wgate_ref, wout_ref, o_ref):
    h = _rms(x_ref[0], nw_ref[...]).astype(BF16)
    gate = _dot(h, wgate_ref[...])
    y = (a_ref[0].astype(F32) * _silu(gate)).astype(BF16)
    o_ref[0] = x_ref[0] + _dot(y, wout_ref[...])


def _att_out(x, att, norm_w, w_gate, w_out):
    B, S, D = x.shape
    ts = ATT_OUT_TILE
    return pl.pallas_call(
        _att_out_kernel,
        name="att_out",
        grid=(B, S // ts),
        in_specs=[
            pl.BlockSpec((1, ts, D), lambda b, t: (b, t, 0)),
            pl.BlockSpec((1, ts, ATT_WIDTH), lambda b, t: (b, t, 0)),
            _const_spec((1, D)),
            _const_spec((D, ATT_WIDTH)),
            _const_spec((ATT_WIDTH, D)),
        ],
        out_specs=pl.BlockSpec((1, ts, D), lambda b, t: (b, t, 0)),
        out_shape=jax.ShapeDtypeStruct((B, S, D), F32),
        compiler_params=pltpu.CompilerParams(
            dimension_semantics=("parallel", "parallel"), vmem_limit_bytes=VMEM_LIMIT_BYTES),
    )(x, att, norm_w.reshape(1, D), w_gate, w_out)


def _att_layer(x, norm_w, w_in, q_norm, k_norm, w_out):
    B, S, D = x.shape
    n_g = len(ATT_GROUPS)
    tables = _att_rope_tables(S)
    perm = _att_head_perm()
    cols = np.arange(w_in.shape[1]).reshape(-1, ATT_HEAD_DIM)
    is_qk = (np.arange(cols.shape[0]) // ATT_HEADS) % 3 < 2
    is_qk &= np.arange(cols.shape[0]) < 3 * n_g * ATT_HEADS
    cols = np.where(is_qk[:, None], cols[:, perm], cols).reshape(-1)
    w_in = w_in[:, cols].astype(BF16)
    q_norm, k_norm = q_norm[:, perm], k_norm[:, perm]
    qkv = []
    for g, (_, dil) in enumerate(ATT_GROUPS):
        w_g = w_in[:, 3 * g * ATT_WIDTH:3 * (g + 1) * ATT_WIDTH]
        qkv.append(_att_proj(x, norm_w, w_g, q_norm[g], k_norm[g], tables, dil))
    att = _att_core(qkv, S)
    return _att_out(x, att, norm_w, w_in[:, 3 * n_g * ATT_WIDTH:], w_out.astype(BF16))


def _ret_rope_tables(S):
    return _rope_angles(S, RET_QK_DIM // 2, RET_QK_DIM, RET_THETA)


def _ret_rotate(val, cos, sin, scale):
    half = RET_QK_DIM // 2
    x1, x2 = val[:, :half], val[:, half:]
    out = jnp.concatenate([x1 * cos - x2 * sin, x2 * cos + x1 * sin], axis=1)
    return out * scale if scale != 1.0 else out


def _ret_decays(lg, reverse):
    C = RET_CHUNK
    i = lax.broadcasted_iota(jnp.int32, (C, C), 0)
    jj = lax.broadcasted_iota(jnp.int32, (C, C), 1)
    col = lax.broadcasted_iota(jnp.int32, (C, 1), 0).astype(F32)
    if reverse:
        dist, keep = jj - i, jj > i
        q_dec = jnp.exp((C - col) * lg)
        k_dec = jnp.exp(col * lg)
    else:
        dist, keep = i - jj, i >= jj
        q_dec = jnp.exp((col + 1.0) * lg)
        k_dec = jnp.exp((C - 1.0 - col) * lg)
    decay = jnp.where(keep, jnp.exp(jnp.where(keep, dist, 0).astype(F32) * lg), 0.0)
    chunk_dec = jnp.exp(jnp.full((1, 1), float(C), F32) * lg)
    return decay, q_dec, k_dec, chunk_dec


def _ret_project_qk(h_s, w_ref, cos_ref, sin_ref, q_s, k_s):
    cos, sin = cos_ref[...], sin_ref[...]
    for hd in range(RET_HEADS):
        qc = slice(hd * RET_QK_DIM, (hd + 1) * RET_QK_DIM)
        kc = slice(RET_QK + hd * RET_QK_DIM, RET_QK + (hd + 1) * RET_QK_DIM)
        q_s[hd] = _ret_rotate(_dot(h_s[...], w_ref[:, qc]), cos, sin, 1.0)
        k_s[hd] = _ret_rotate(_dot(h_s[...], w_ref[:, kc]), cos, sin, RET_QK_DIM ** -0.5)


def _ret_scan_chunk(rows, lg_row, q_s, k_s, v_at, st_s, reverse, filler=None):
    heads = range(RET_HEADS)
    decs = [_ret_decays(lg_row(hd), reverse) for hd in heads]
    q = [q_s[hd, rows, :] for hd in heads]
    k = [k_s[hd, rows, :] for hd in heads]
    v = [v_at(hd) for hd in heads]
    inner = [_dot_nt(q[hd].astype(BF16), k[hd].astype(BF16)) for hd in heads]
    kv = [_dot_tn((k[hd] * decs[hd][2]).astype(BF16), v[hd]) for hd in heads]
    extra = filler() if filler is not None else None
    outs = []
    for hd in heads:
        decay, q_dec, _, chunk_dec = decs[hd]
        state = st_s[hd]
        outs.append(_dot((inner[hd] * decay).astype(BF16), v[hd])
                    + _dot((q[hd] * q_dec).astype(BF16), state.astype(BF16)))
        st_s[hd] = state * chunk_dec + kv[hd]
    return outs, extra


def _ret_fwd_kernel(lg_ref, x_ref, nw_ref, w_ref, cos_ref, sin_ref, f_ref, v_ref,
                    h_s, q_s, k_s, st_s, *, ts):
    @pl.when(pl.program_id(1) == 0)
    def _():
        st_s[...] = jnp.zeros_like(st_s)

    h_s[...] = _rms(x_ref[0], nw_ref[...]).astype(BF16)
    _ret_project_qk(h_s, w_ref, cos_ref, sin_ref, q_s, k_s)
    for hd in range(RET_HEADS):
        vc = slice(2 * RET_QK + hd * RET_V_DIM, 2 * RET_QK + (hd + 1) * RET_V_DIM)
        v_ref[0, :, hd * RET_V_DIM:(hd + 1) * RET_V_DIM] = _dot(h_s[...], w_ref[:, vc]).astype(BF16)
    for c in range(ts // RET_CHUNK):
        rows = slice(c * RET_CHUNK, (c + 1) * RET_CHUNK)
        outs, _ = _ret_scan_chunk(
            rows, lambda hd: lg_ref[0, hd], q_s, k_s,
            lambda hd: v_ref[0, rows, hd * RET_V_DIM:(hd + 1) * RET_V_DIM], st_s, False)
        for hd in range(RET_HEADS):
            f_ref[0, rows, hd * RET_V_DIM:(hd + 1) * RET_V_DIM] = outs[hd]


def _ret_bwd_kernel(lg_ref, x_ref, nw_ref, w_ref, cos_ref, sin_ref, f_ref, v_ref, wout_ref,
                    o_ref, h_s, q_s, k_s, y_s, st_s, *, ts):
    @pl.when(pl.program_id(1) == 0)
    def _():
        st_s[...] = jnp.zeros_like(st_s)

    h_s[...] = _rms(x_ref[0], nw_ref[...]).astype(BF16)
    _ret_project_qk(h_s, w_ref, cos_ref, sin_ref, q_s, k_s)
    for c in reversed(range(ts // RET_CHUNK)):
        rows = slice(c * RET_CHUNK, (c + 1) * RET_CHUNK)

        def gates():
            return [_dot(h_s[rows, :], w_ref[:, 2 * RET_QK + hd * RET_V_DIM:
                                             2 * RET_QK + (hd + 1) * RET_V_DIM])
                    for hd in range(RET_HEADS)]

        outs, gate = _ret_scan_chunk(
            rows, lambda hd: lg_ref[1, hd], q_s, k_s,
            lambda hd: v_ref[0, rows, hd * RET_V_DIM:(hd + 1) * RET_V_DIM], st_s, True, gates)
        for hd in range(RET_HEADS):
            oc = slice(hd * RET_V_DIM, (hd + 1) * RET_V_DIM)
            y = outs[hd] + f_ref[0, rows, oc]
            y = y * lax.rsqrt(jnp.mean(y * y, axis=-1, keepdims=True) + EPS)
            y_s[rows, oc] = (y * _silu(gate[hd])).astype(BF16)
    o_ref[0] = x_ref[0] + _dot(y_s[...], wout_ref[...])


def _ret_layer(x, norm_w, w_in, decay_exp, w_out):
    B, S, D = x.shape
    ts = RET_TILE
    nt = S // ts
    w_in = w_in.astype(BF16)
    w_fwd = jnp.concatenate([w_in[:, :2 * RET_QK], w_in[:, 4 * RET_QK:4 * RET_QK + RET_V]], axis=1)
    w_bwd = jnp.concatenate([w_in[:, 2 * RET_QK:4 * RET_QK], w_in[:, 4 * RET_QK + RET_V:]], axis=1)
    log_gamma = jnp.log1p(-jnp.exp2(-decay_exp.astype(F32)))
    cos, sin = _ret_rope_tables(S)
    half = RET_QK_DIM // 2
    smem = pl.BlockSpec(memory_space=pltpu.SMEM)
    cparams = pltpu.CompilerParams(
        dimension_semantics=("parallel", "arbitrary"), vmem_limit_bytes=VMEM_LIMIT_BYTES)
    state = pltpu.VMEM((RET_HEADS, RET_QK_DIM, RET_V_DIM), F32)
    qk_scratch = pltpu.VMEM((RET_HEADS, ts, RET_QK_DIM), F32)

    fwd, v = pl.pallas_call(
        functools.partial(_ret_fwd_kernel, ts=ts),
        name="ret_fwd",
        grid=(B, nt),
        in_specs=[
            smem,
            pl.BlockSpec((1, ts, D), lambda b, t: (b, t, 0)),
            _const_spec((1, D)),
            _const_spec((D, 2 * RET_QK + RET_V)),
            pl.BlockSpec((ts, half), lambda b, t: (t, 0)),
            pl.BlockSpec((ts, half), lambda b, t: (t, 0)),
        ],
        out_specs=[pl.BlockSpec((1, ts, RET_V), lambda b, t: (b, t, 0)),
                   pl.BlockSpec((1, ts, RET_V), lambda b, t: (b, t, 0))],
        out_shape=[jax.ShapeDtypeStruct((B, S, RET_V), F32),
                   jax.ShapeDtypeStruct((B, S, RET_V), BF16)],
        scratch_shapes=[pltpu.VMEM((ts, D), BF16), qk_scratch, qk_scratch, state],
        compiler_params=cparams,
    )(log_gamma, x, norm_w.reshape(1, D), w_fwd, cos, sin)

    rev = lambda b, t: (b, nt - 1 - t, 0)
    return pl.pallas_call(
        functools.partial(_ret_bwd_kernel, ts=ts),
        name="ret_bwd",
        grid=(B, nt),
        in_specs=[
            smem,
            pl.BlockSpec((1, ts, D), rev),
            _const_spec((1, D)),
            _const_spec((D, 2 * RET_QK + RET_V)),
            pl.BlockSpec((ts, half), lambda b, t: (nt - 1 - t, 0)),
            pl.BlockSpec((ts, half), lambda b, t: (nt - 1 - t, 0)),
            pl.BlockSpec((1, ts, RET_V), rev),
            pl.BlockSpec((1, ts, RET_V), rev),
            _const_spec((RET_V, D)),
        ],
        out_specs=pl.BlockSpec((1, ts, D), rev),
        out_shape=jax.ShapeDtypeStruct((B, S, D), F32),
        scratch_shapes=[pltpu.VMEM((ts, D), BF16), qk_scratch, qk_scratch,
                        pltpu.VMEM((ts, RET_V), BF16), state],
        compiler_params=cparams,
    )(log_gamma, x, norm_w.reshape(1, D), w_bwd, cos, sin, fwd, v, w_out.astype(BF16))


def kernel(x, pool_norm, pool_w_in, pool_w_group, pool_scale, pool_w_out,
           att_norm, att_w_in, att_q_norm, att_k_norm, att_w_out,
           ret_norm, ret_w_in, ret_decay, ret_w_out):
    depth = pool_norm.shape[0] + att_norm.shape[0] + ret_norm.shape[0]
    for layer in range(depth):
        kind, idx = layer % 3, layer // 3
        if kind == 0:
            x = _pool_layer(x, pool_norm[idx], pool_w_in[idx], pool_w_group[idx], pool_scale[idx],
                            pool_w_out[idx])
        elif kind == 1:
            x = _att_layer(x, att_norm[idx], att_w_in[idx], att_q_norm[idx], att_k_norm[idx],
                           att_w_out[idx])
        else:
            x = _ret_layer(x, ret_norm[idx], ret_w_in[idx], ret_decay[idx], ret_w_out[idx])
    return x
```

```python
import functools
import math

import jax
import jax.numpy as jnp
import numpy as np
from jax import lax
from jax.experimental import pallas as pl
from jax.experimental.pallas import tpu as pltpu

F32 = jnp.float32
BF16 = jnp.bfloat16

D_MODEL = 1024
EPS = 1e-6
NEG_BIG = -1e30
LOG2E = math.log2(math.e)
LN2 = math.log(2.0)

POOL_WINDOWS = (2, 4, 8, 16)
POOL_WIDTH = 2 * D_MODEL
POOL_GROUP = POOL_WIDTH // len(POOL_WINDOWS)

ATT_GROUPS = ((128, 1), (512, 4), (2048, 16))
ATT_HEAD_DIM = 128
ATT_HEADS = D_MODEL // ATT_HEAD_DIM
ATT_WIDTH = ATT_HEADS * ATT_HEAD_DIM
ATT_SIDE = 64
ROPE_DIM = ATT_HEAD_DIM // 4
ROPE_THETA = 500000.0

RET_HEADS = 4
RET_QK_DIM = D_MODEL // RET_HEADS
RET_V_DIM = 2 * D_MODEL // RET_HEADS
RET_QK = RET_HEADS * RET_QK_DIM
RET_V = RET_HEADS * RET_V_DIM
RET_CHUNK = 256
RET_THETA = 10000.0

V7X_VMEM_BYTES = 64 * 1024 * 1024
VMEM_LIMIT_BYTES = V7X_VMEM_BYTES - 8 * 1024 * 1024
LANES = 128
BF16_ROWS = 16

POOL_TILE = 512
POOL_HALO = BF16_ROWS
POOL_BLOCK = 128
ATT_PROJ_TILE = 1024
ATT_TILE = 2048
ATT_QBLOCK = 128
ATT_OUT_TILE = 1024
ATT_STRIDE_STEP = 4
ATT_MERGE_ROWS = 32
ATT_PAIRS = ATT_HEADS // 2
ATT_PAIR_W = 2 * ATT_HEAD_DIM
RET_TILE = 512


def _const_spec(shape):
    nd = len(shape)
    return pl.BlockSpec(shape, lambda *_: (0,) * nd, pipeline_mode=pl.Buffered(1))


def _rms(v, w):
    return v * lax.rsqrt(jnp.mean(v * v, axis=-1, keepdims=True) + EPS) * w


def _silu(g):
    return g * (1.0 / (1.0 + jnp.exp(-g)))


def _rope_angles(S, half, rot_dim, theta):
    inv_freq = (np.float32(1.0) / np.power(np.float32(theta),
                                           np.arange(half, dtype=np.float32) * np.float32(2.0 / rot_dim)))
    ang = np.arange(S, dtype=np.float32)[:, None] * inv_freq[None, :]
    return np.cos(ang).astype(np.float32), np.sin(ang).astype(np.float32)


def _dot(a, b):
    return jnp.dot(a, b, preferred_element_type=F32)


def _dot_nt(a, b):
    return lax.dot_general(a, b, (((1,), (1,)), ((), ())), preferred_element_type=F32)


def _dot_tn(a, b):
    return lax.dot_general(a, b, (((0,), (0,)), ((), ())), preferred_element_type=F32)


def _pool_kernel(x_ref, xp_ref, xn_ref, nw_ref, win_ref, wg_ref, sc_ref, wout_ref, band_ref,
                 o_ref, h_s, u_s, ub_s, d_s, y_s, *, ts, seq):
    t = pl.program_id(1)
    nt = pl.num_programs(1)
    nw = nw_ref[...]
    halo = POOL_HALO
    n_grp = len(POOL_WINDOWS)

    h_s[halo:halo + ts, :] = _rms(x_ref[0], nw).astype(BF16)
    hp = _rms(xp_ref[0], nw)
    h_s[0:halo, :] = jnp.where(t > 0, hp, 0.0).astype(BF16)
    hn = _rms(xn_ref[0], nw)
    h_s[halo + ts:, :] = jnp.where(t < nt - 1, hn, 0.0).astype(BF16)

    row = lax.broadcasted_iota(jnp.int32, (POOL_BLOCK, 1), 0)

    def project(g):
        ug = _dot(h_s[...], win_ref[:, g * POOL_GROUP:(g + 1) * POOL_GROUP])
        u_s[g] = ug
        ub_s[g] = ug.astype(BF16)

    def pool(g):
        window = POOL_WINDOWS[g]
        for b in range(ts // POOL_BLOCK):
            r0 = b * POOL_BLOCK
            wsum = _dot(band_ref[g], ub_s[g, r0:r0 + POOL_BLOCK + 2 * halo, :])
            pos = t * ts + r0 + row
            lo = jnp.maximum(pos - window // 2, 0)
            hi = jnp.minimum(pos - window // 2 + window, seq)
            diff = wsum / (hi - lo).astype(F32) - u_s[g, halo + r0:halo + r0 + POOL_BLOCK, :]
            d_s[g, r0:r0 + POOL_BLOCK, :] = diff.astype(BF16)

    def mix(g):
        cols = slice(g * POOL_GROUP, (g + 1) * POOL_GROUP)
        gate = _dot(h_s[halo:halo + ts, :], win_ref[:, POOL_WIDTH + g * POOL_GROUP:
                                                    POOL_WIDTH + (g + 1) * POOL_GROUP])
        mixed = _dot(d_s[g], wg_ref[g]) * sc_ref[:, cols]
        y_s[:, cols] = (mixed * _silu(gate)).astype(BF16)

    for step in range(n_grp + 2):
        if step < n_grp:
            project(step)
        if 0 <= step - 1 < n_grp:
            pool(step - 1)
        if 0 <= step - 2 < n_grp:
            mix(step - 2)
    o_ref[0] = x_ref[0] + _dot(y_s[...], wout_ref[...])


def _pool_band():
    i = np.arange(POOL_BLOCK)[:, None]
    j = np.arange(POOL_BLOCK + 2 * POOL_HALO)[None, :] - POOL_HALO
    bands = [((j - i >= -(w // 2)) & (j - i < w - w // 2)) for w in POOL_WINDOWS]
    return jnp.asarray(np.stack(bands).astype(np.float32), dtype=BF16)


def _pool_layer(x, norm_w, w_in, w_group, scale, w_out):
    B, S, D = x.shape
    ts, halo = POOL_TILE, POOL_HALO
    nt = S // ts
    hb = ts // halo
    n_hb = S // halo
    kern = functools.partial(_pool_kernel, ts=ts, seq=S)
    return pl.pallas_call(
        kern,
        name="pool_layer",
        grid=(B, nt),
        in_specs=[
            pl.BlockSpec((1, ts, D), lambda b, t: (b, t, 0)),
            pl.BlockSpec((1, halo, D), lambda b, t: (b, jnp.maximum(t * hb - 1, 0), 0)),
            pl.BlockSpec((1, halo, D), lambda b, t: (b, jnp.minimum((t + 1) * hb, n_hb - 1), 0)),
            _const_spec((1, D)),
            _const_spec((D, 2 * POOL_WIDTH)),
            _const_spec((len(POOL_WINDOWS), POOL_GROUP, POOL_GROUP)),
            _const_spec((1, POOL_WIDTH)),
            _const_spec((POOL_WIDTH, D)),
            _const_spec((len(POOL_WINDOWS), POOL_BLOCK, POOL_BLOCK + 2 * halo)),
        ],
        out_specs=pl.BlockSpec((1, ts, D), lambda b, t: (b, t, 0)),
        out_shape=jax.ShapeDtypeStruct((B, S, D), F32),
        scratch_shapes=[
            pltpu.VMEM((ts + 2 * halo, D), BF16),
            pltpu.VMEM((len(POOL_WINDOWS), ts + 2 * halo, POOL_GROUP), F32),
            pltpu.VMEM((len(POOL_WINDOWS), ts + 2 * halo, POOL_GROUP), BF16),
            pltpu.VMEM((len(POOL_WINDOWS), ts, POOL_GROUP), BF16),
            pltpu.VMEM((ts, POOL_WIDTH), BF16),
        ],
        compiler_params=pltpu.CompilerParams(
            dimension_semantics=("parallel", "parallel"), vmem_limit_bytes=VMEM_LIMIT_BYTES),
    )(x, x, x, norm_w.reshape(1, D), w_in.astype(BF16), w_group.astype(BF16),
      scale.reshape(1, POOL_WIDTH), w_out.astype(BF16), _pool_band())


def _att_head_perm():
    half = ROPE_DIM // 2
    rest = (ATT_HEAD_DIM - ROPE_DIM) // 2
    order = (list(range(half)) + list(range(ROPE_DIM, ROPE_DIM + rest))
             + list(range(half, ROPE_DIM)) + list(range(ROPE_DIM + rest, ATT_HEAD_DIM)))
    return np.array(order, dtype=np.int32)


def _att_rope_tables(S):
    half = ROPE_DIM // 2
    cos, sin = _rope_angles(S, half, ROPE_DIM, ROPE_THETA)
    pad = ATT_HEAD_DIM // 2 - half
    one, zero = np.ones((S, pad), np.float32), np.zeros((S, pad), np.float32)
    return (np.concatenate([cos, one, cos, one], axis=1),
            np.concatenate([-sin, zero, sin, zero], axis=1))


def _att_proj_kernel(x_ref, nw_ref, w_ref, qn_ref, kn_ref, hsum_ref, cos_ref, sin_ref,
                     q_ref, k_ref, v_ref, h_s, hf_s, hg_s, *, ts, dil):
    hd = ATT_HEAD_DIM
    n = ts // dil
    hn = _rms(x_ref[0], nw_ref[...])
    if dil == 1:
        h_s[...] = hn.astype(BF16)
    else:
        n_slab = hn.shape[1] // LANES
        for c in range(n_slab):
            hf_s[c] = hn[:, c * LANES:(c + 1) * LANES]
        if dil > ATT_STRIDE_STEP:
            step = ATT_STRIDE_STEP
            for c in range(n_slab):
                for lo in range(step):
                    hg_s[c, lo] = hf_s[c, pl.ds(lo, ts // step, stride=step), :]
            for r in range(dil):
                hi, lo = divmod(r, step)
                rows = pl.ds(hi, n, stride=dil // step)
                h_s[r * n:(r + 1) * n, :] = jnp.concatenate(
                    [hg_s[c, lo, rows, :] for c in range(n_slab)], axis=1).astype(BF16)
        else:
            for r in range(dil):
                rows = pl.ds(r, n, stride=dil)
                h_s[r * n:(r + 1) * n, :] = jnp.concatenate(
                    [hf_s[c, rows, :] for c in range(n_slab)], axis=1).astype(BF16)
    cos_t, sin_t = cos_ref[...], sin_ref[...]

    def emit(val, out_ref, pair, hh):
        out_ref[0, pair, :, :, hh * hd:(hh + 1) * hd] = val.astype(BF16).reshape(dil, n, hd)

    def qk_head(val, inv_rms, gain):
        y = val * inv_rms * gain
        return y * cos_t + pltpu.roll(y, hd // 2, 1) * sin_t

    gains = (qn_ref[...] * (hd ** -0.5 * LOG2E), kn_ref[...])
    out_refs = (q_ref, k_ref, v_ref)
    n_pairs = 3 * ATT_PAIRS

    def project(i):
        return _dot(h_s[...], w_ref[:, i * ATT_PAIR_W:(i + 1) * ATT_PAIR_W])

    nxt = project(0)
    for i in range(n_pairs):
        proj = nxt
        if i + 1 < n_pairs:
            nxt = project(i + 1)
        kind, pair = divmod(i, ATT_PAIRS)
        if kind < 2:
            ssq = _dot((proj * proj).astype(BF16), hsum_ref[...])
            inv_rms = lax.rsqrt(ssq * (1.0 / hd) + EPS)
        for hh in range(2):
            lanes = slice(hh * hd, (hh + 1) * hd)
            if kind < 2:
                emit(qk_head(proj[:, lanes], inv_rms[:, lanes], gains[kind]), out_refs[kind], pair, hh)
            else:
                emit(proj[:, lanes], v_ref, pair, hh)


def _att_proj(x, norm_w, w_qkv, q_gain, k_gain, tables, dil):
    B, S, D = x.shape
    ts = ATT_PROJ_TILE
    n = ts // dil
    kern = functools.partial(_att_proj_kernel, ts=ts, dil=dil)
    tables = [tab.reshape(S // ts, n, dil, ATT_HEAD_DIM).transpose(0, 2, 1, 3).reshape(S, ATT_HEAD_DIM)
              for tab in tables]
    tab_spec = pl.BlockSpec((ts, ATT_HEAD_DIM), lambda b, t: (t, 0))
    out_spec = pl.BlockSpec((1, ATT_PAIRS, dil, n, ATT_PAIR_W), lambda b, t: (b, 0, 0, t, 0))
    out_sds = jax.ShapeDtypeStruct((B, ATT_PAIRS, dil, S // dil, ATT_PAIR_W), BF16)
    head_sum = jnp.asarray(np.kron(np.eye(2, dtype=np.float32),
                                   np.ones((ATT_HEAD_DIM, ATT_HEAD_DIM), np.float32)), dtype=BF16)
    return pl.pallas_call(
        kern,
        name=f"att_proj_d{dil}",
        grid=(B, S // ts),
        in_specs=[
            pl.BlockSpec((1, ts, D), lambda b, t: (b, t, 0)),
            _const_spec((1, D)),
            _const_spec((D, 3 * ATT_WIDTH)),
            _const_spec((1, ATT_HEAD_DIM)),
            _const_spec((1, ATT_HEAD_DIM)),
            _const_spec((ATT_PAIR_W, ATT_PAIR_W)),
            tab_spec, tab_spec,
        ],
        out_specs=[out_spec, out_spec, out_spec],
        out_shape=[out_sds, out_sds, out_sds],
        scratch_shapes=[pltpu.VMEM((ts, D), BF16), pltpu.VMEM((D // LANES, ts, LANES), F32),
                        pltpu.VMEM((D // LANES, ATT_STRIDE_STEP, ts // ATT_STRIDE_STEP, LANES), F32)],
        compiler_params=pltpu.CompilerParams(
            dimension_semantics=("parallel", "parallel"), vmem_limit_bytes=VMEM_LIMIT_BYTES),
    )(x, norm_w.reshape(1, D), w_qkv, q_gain.reshape(1, -1), k_gain.reshape(1, -1), head_sum,
      *tables)


def _att_core_kernel(*refs, ts, seq):
    n_g = len(ATT_GROUPS)
    grp = [refs[7 * g:7 * g + 7] for g in range(n_g)]
    o_ref = refs[7 * n_g]
    og_s, lse_s = refs[7 * n_g + 1:]
    t = pl.program_id(1)
    side = ATT_SIDE
    hd = ATT_HEAD_DIM
    qb = ATT_QBLOCK
    nk = qb + 2 * side

    for g, (_, dil) in enumerate(ATT_GROUPS):
        q_ref, km_ref, kp_ref, kn_ref, vm_ref, vp_ref, vn_ref = grp[g]
        n = ts // dil
        n_sub = seq // dil

        def window(main_ref, prev_ref, next_ref, r, c0):
            parts = []
            if c0 == 0:
                parts.append(prev_ref[0, 0, r])
                lo = 0
            else:
                lo = c0 - side
            hi = min(c0 + qb + side, n)
            parts.append(main_ref[0, 0, r, lo:hi, :])
            if c0 + qb + side > n:
                parts.append(next_ref[0, 0, r])
            return parts[0] if len(parts) == 1 else jnp.concatenate(parts, axis=0)

        @pl.when(t >= 0)
        def _():
            a = lax.broadcasted_iota(jnp.int32, (qb, nk), 0)
            j = lax.broadcasted_iota(jnp.int32, (qb, nk), 1)
            band_bias = jnp.where((j - a >= 0) & (j - a <= 2 * side), 0.0, NEG_BIG).astype(F32)
            jrow = lax.broadcasted_iota(jnp.int32, (1, nk), 1)
            ones = jnp.ones((nk, hd), BF16)
            for r in range(dil):
                for c0 in range(0, n, qb):
                    base = t * n + c0 - side
                    key_ok = (jrow + base >= 0) & (jrow + base < n_sub)
                    bias = band_bias + jnp.where(key_ok, 0.0, NEG_BIG).astype(F32)
                    qv = q_ref[0, 0, r, c0:c0 + qb, :]
                    kv = window(km_ref, kp_ref, kn_ref, r, c0)
                    vv = window(vm_ref, vp_ref, vn_ref, r, c0)
                    for hh in range(2):
                        cols = slice(hh * hd, (hh + 1) * hd)
                        s = _dot_nt(qv[:, cols], kv[:, cols]) + bias
                        m = jnp.max(s, axis=-1, keepdims=True)
                        p = jnp.exp2(s - m).astype(BF16)
                        acc = _dot(p, jnp.concatenate([vv[:, cols], ones], axis=1))
                        den = acc[:, hd:]
                        o = acc[:, :hd] / den
                        lse = m * LN2 + jnp.log(den)
                        if dil == 1:
                            rows = slice(c0, c0 + qb)
                        else:
                            rows = pl.ds(c0 * dil + r, qb, stride=dil)
                        og_s[g, hh, rows, :] = o
                        lse_s[g, hh, rows, :] = lse

    for c0 in range(0, ts, ATT_MERGE_ROWS):
        rows = slice(c0, c0 + ATT_MERGE_ROWS)
        ys = []
        for hh in range(2):
            lses = [lse_s[g, hh, rows, :] for g in range(n_g)]
            m = functools.reduce(jnp.maximum, lses)
            ws = [jnp.exp(l - m) for l in lses]
            den = functools.reduce(lambda u, v: u + v, ws)
            num = functools.reduce(lambda u, v: u + v,
                                   [w * og_s[g, hh, rows, :] for g, w in enumerate(ws)])
            ys.append(num / den)
        o_ref[0, rows, :] = jnp.concatenate(ys, axis=1).astype(BF16)


def _att_core(qkv, S):
    B = qkv[0][0].shape[0]
    ts = ATT_TILE
    side = ATT_SIDE
    pw = ATT_PAIR_W
    in_specs, args = [], []
    for (_, dil), (q, k, v) in zip(ATT_GROUPS, qkv):
        n = ts // dil
        per = n // side
        last = S // dil // side - 1
        main = pl.BlockSpec((1, 1, dil, n, pw), lambda b, t, h: (b, h, 0, t, 0))
        prev = pl.BlockSpec((1, 1, dil, side, pw),
                            lambda b, t, h, per=per: (b, h, 0, jnp.maximum(t * per - 1, 0), 0))
        nxt = pl.BlockSpec((1, 1, dil, side, pw),
                           lambda b, t, h, per=per, last=last: (b, h, 0, jnp.minimum((t + 1) * per, last), 0))
        in_specs += [main, main, prev, nxt, main, prev, nxt]
        args += [q, k, k, k, v, v, v]
    kern = functools.partial(_att_core_kernel, ts=ts, seq=S)
    n_g = len(ATT_GROUPS)
    return pl.pallas_call(
        kern,
        name="att_core",
        grid=(B, S // ts, ATT_PAIRS),
        in_specs=in_specs,
        out_specs=pl.BlockSpec((1, ts, pw), lambda b, t, h: (b, t, h)),
        out_shape=jax.ShapeDtypeStruct((B, S, ATT_WIDTH), BF16),
        scratch_shapes=[
            pltpu.VMEM((n_g, 2, ts, ATT_HEAD_DIM), F32),
            pltpu.VMEM((n_g, 2, ts, ATT_HEAD_DIM), F32),
        ],
        compiler_params=pltpu.CompilerParams(
            dimension_semantics=("parallel", "parallel", "parallel"),
            vmem_limit_bytes=VMEM_LIMIT_BYTES),
    )(*args)


def _att_out_kernel(x_ref, a_ref, nw_ref, wgate_ref, wout_ref, o_ref):
    h = _rms(x_ref[0], nw_ref[...]).astype(BF16)
    gate = _dot(h, wgate_ref[...])
    y = (a_ref[0].astype(F32) * _silu(gate)).astype(BF16)
    o_ref[0] = x_ref[0] + _dot(y, wout_ref[...])


def _att_out(x, att, norm_w, w_gate, w_out):
    B, S, D = x.shape
    ts = ATT_OUT_TILE
    return pl.pallas_call(
        _att_out_kernel,
        name="att_out",
        grid=(B, S // ts),
        in_specs=[
            pl.BlockSpec((1, ts, D), lambda b, t: (b, t, 0)),
            pl.BlockSpec((1, ts, ATT_WIDTH), lambda b, t: (b, t, 0)),
            _const_spec((1, D)),
            _const_spec((D, ATT_WIDTH)),
            _const_spec((ATT_WIDTH, D)),
        ],
        out_specs=pl.BlockSpec((1, ts, D), lambda b, t: (b, t, 0)),
        out_shape=jax.ShapeDtypeStruct((B, S, D), F32),
        compiler_params=pltpu.CompilerParams(
            dimension_semantics=("parallel", "parallel"), vmem_limit_bytes=VMEM_LIMIT_BYTES),
    )(x, att, norm_w.reshape(1, D), w_gate, w_out)


def _att_layer(x, norm_w, w_in, q_norm, k_norm, w_out):
    B, S, D = x.shape
    n_g = len(ATT_GROUPS)
    tables = _att_rope_tables(S)
    perm = _att_head_perm()
    head = np.arange(w_in.shape[1] // ATT_HEAD_DIM)
    is_qk = ((head // ATT_HEADS) % 3 < 2) & (head < 3 * n_g * ATT_HEADS)
    w_heads = w_in.reshape(D, -1, ATT_HEAD_DIM)
    cuts = [0, ROPE_DIM // 2, ROPE_DIM, ROPE_DIM + (ATT_HEAD_DIM - ROPE_DIM) // 2, ATT_HEAD_DIM]
    parts = [w_heads[..., lo:hi] for lo, hi in zip(cuts[:-1], cuts[1:])]
    w_perm = jnp.concatenate([parts[0], parts[2], parts[1], parts[3]], axis=-1)
    w_in = jnp.where(is_qk[None, :, None], w_perm, w_heads).astype(BF16).reshape(D, -1)
    q_norm, k_norm = q_norm[:, perm], k_norm[:, perm]
    qkv = []
    for g, (_, dil) in enumerate(ATT_GROUPS):
        w_g = w_in[:, 3 * g * ATT_WIDTH:3 * (g + 1) * ATT_WIDTH]
        qkv.append(_att_proj(x, norm_w, w_g, q_norm[g], k_norm[g], tables, dil))
    att = _att_core(qkv, S)
    return _att_out(x, att, norm_w, w_in[:, 3 * n_g * ATT_WIDTH:], w_out.astype(BF16))


def _ret_rope_tables(S):
    return _rope_angles(S, RET_QK_DIM // 2, RET_QK_DIM, RET_THETA)


def _ret_rotate(val, cos, sin, scale):
    half = RET_QK_DIM // 2
    x1, x2 = val[:, :half], val[:, half:]
    out = jnp.concatenate([x1 * cos - x2 * sin, x2 * cos + x1 * sin], axis=1)
    return out * scale if scale != 1.0 else out


def _ret_decays(lg, reverse):
    C = RET_CHUNK
    i = lax.broadcasted_iota(jnp.int32, (C, C), 0)
    jj = lax.broadcasted_iota(jnp.int32, (C, C), 1)
    col = lax.broadcasted_iota(jnp.int32, (C, 1), 0).astype(F32)
    if reverse:
        dist, keep = jj - i, jj > i
        q_dec = jnp.exp((C - col) * lg)
        k_dec = jnp.exp(col * lg)
    else:
        dist, keep = i - jj, i >= jj
        q_dec = jnp.exp((col + 1.0) * lg)
        k_dec = jnp.exp((C - 1.0 - col) * lg)
    decay = jnp.where(keep, jnp.exp(jnp.where(keep, dist, 0).astype(F32) * lg), 0.0)
    chunk_dec = jnp.exp(jnp.full((1, 1), float(C), F32) * lg)
    return decay, q_dec, k_dec, chunk_dec


def _ret_project_qk(h_s, w_ref, cos_ref, sin_ref, q_s, k_s):
    cos, sin = cos_ref[...], sin_ref[...]
    for hd in range(RET_HEADS):
        qc = slice(hd * RET_QK_DIM, (hd + 1) * RET_QK_DIM)
        kc = slice(RET_QK + hd * RET_QK_DIM, RET_QK + (hd + 1) * RET_QK_DIM)
        q_s[hd] = _ret_rotate(_dot(h_s[...], w_ref[:, qc]), cos, sin, 1.0)
        k_s[hd] = _ret_rotate(_dot(h_s[...], w_ref[:, kc]), cos, sin, RET_QK_DIM ** -0.5)


def _ret_scan_chunk(rows, lg_row, q_s, k_s, v_at, st_s, reverse, filler=None):
    heads = range(RET_HEADS)
    decs = [_ret_decays(lg_row(hd), reverse) for hd in heads]
    q = [q_s[hd, rows, :] for hd in heads]
    k = [k_s[hd, rows, :] for hd in heads]
    v = [v_at(hd) for hd in heads]
    inner = [_dot_nt(q[hd].astype(BF16), k[hd].astype(BF16)) for hd in heads]
    kv = [_dot_tn((k[hd] * decs[hd][2]).astype(BF16), v[hd]) for hd in heads]
    extra = filler() if filler is not None else None
    outs = []
    for hd in heads:
        decay, q_dec, _, chunk_dec = decs[hd]
        state = st_s[hd]
        outs.append(_dot((inner[hd] * decay).astype(BF16), v[hd])
                    + _dot((q[hd] * q_dec).astype(BF16), state.astype(BF16)))
        st_s[hd] = state * chunk_dec + kv[hd]
    return outs, extra


def _ret_fwd_kernel(lg_ref, x_ref, nw_ref, w_ref, wv_ref, cos_ref, sin_ref, f_ref, v_ref,
                    h_s, q_s, k_s, st_s, *, ts):
    @pl.when(pl.program_id(1) == 0)
    def _():
        st_s[...] = jnp.zeros_like(st_s)

    h_s[...] = _rms(x_ref[0], nw_ref[...]).astype(BF16)
    _ret_project_qk(h_s, w_ref, cos_ref, sin_ref, q_s, k_s)
    for hd in range(RET_HEADS):
        vc = slice(hd * RET_V_DIM, (hd + 1) * RET_V_DIM)
        v_ref[0, :, vc] = _dot(h_s[...], wv_ref[:, vc]).astype(BF16)
    for c in range(ts // RET_CHUNK):
        rows = slice(c * RET_CHUNK, (c + 1) * RET_CHUNK)
        outs, _ = _ret_scan_chunk(
            rows, lambda hd: lg_ref[0, hd], q_s, k_s,
            lambda hd: v_ref[0, rows, hd * RET_V_DIM:(hd + 1) * RET_V_DIM], st_s, False)
        for hd in range(RET_HEADS):
            f_ref[0, rows, hd * RET_V_DIM:(hd + 1) * RET_V_DIM] = outs[hd]


def _ret_bwd_kernel(lg_ref, x_ref, nw_ref, w_ref, wg_ref, cos_ref, sin_ref, f_ref, v_ref, wout_ref,
                    o_ref, h_s, q_s, k_s, y_s, st_s, *, ts):
    @pl.when(pl.program_id(1) == 0)
    def _():
        st_s[...] = jnp.zeros_like(st_s)

    h_s[...] = _rms(x_ref[0], nw_ref[...]).astype(BF16)
    _ret_project_qk(h_s, w_ref, cos_ref, sin_ref, q_s, k_s)
    for c in reversed(range(ts // RET_CHUNK)):
        rows = slice(c * RET_CHUNK, (c + 1) * RET_CHUNK)

        def gates():
            return [_dot(h_s[rows, :], wg_ref[:, hd * RET_V_DIM:(hd + 1) * RET_V_DIM])
                    for hd in range(RET_HEADS)]

        outs, gate = _ret_scan_chunk(
            rows, lambda hd: lg_ref[1, hd], q_s, k_s,
            lambda hd: v_ref[0, rows, hd * RET_V_DIM:(hd + 1) * RET_V_DIM], st_s, True, gates)
        for hd in range(RET_HEADS):
            oc = slice(hd * RET_V_DIM, (hd + 1) * RET_V_DIM)
            y = outs[hd] + f_ref[0, rows, oc]
            y = y * lax.rsqrt(jnp.mean(y * y, axis=-1, keepdims=True) + EPS)
            y_s[rows, oc] = (y * _silu(gate[hd])).astype(BF16)
    o_ref[0] = x_ref[0] + _dot(y_s[...], wout_ref[...])


def _ret_layer(x, norm_w, w_in, decay_exp, w_out):
    B, S, D = x.shape
    ts = RET_TILE
    nt = S // ts
    w_in = w_in.astype(BF16)

    def w_cols(j):
        return pl.BlockSpec((D, 2 * RET_QK), lambda b, t: (0, j), pipeline_mode=pl.Buffered(1))

    log_gamma = jnp.log1p(-jnp.exp2(-decay_exp.astype(F32)))
    cos, sin = _ret_rope_tables(S)
    half = RET_QK_DIM // 2
    smem = pl.BlockSpec(memory_space=pltpu.SMEM)
    cparams = pltpu.CompilerParams(
        dimension_semantics=("parallel", "arbitrary"), vmem_limit_bytes=VMEM_LIMIT_BYTES)
    state = pltpu.VMEM((RET_HEADS, RET_QK_DIM, RET_V_DIM), F32)
    qk_scratch = pltpu.VMEM((RET_HEADS, ts, RET_QK_DIM), F32)

    fwd, v = pl.pallas_call(
        functools.partial(_ret_fwd_kernel, ts=ts),
        name="ret_fwd",
        grid=(B, nt),
        in_specs=[
            smem,
            pl.BlockSpec((1, ts, D), lambda b, t: (b, t, 0)),
            _const_spec((1, D)),
            w_cols(0), w_cols(2),
            pl.BlockSpec((ts, half), lambda b, t: (t, 0)),
            pl.BlockSpec((ts, half), lambda b, t: (t, 0)),
        ],
        out_specs=[pl.BlockSpec((1, ts, RET_V), lambda b, t: (b, t, 0)),
                   pl.BlockSpec((1, ts, RET_V), lambda b, t: (b, t, 0))],
        out_shape=[jax.ShapeDtypeStruct((B, S, RET_V), F32),
                   jax.ShapeDtypeStruct((B, S, RET_V), BF16)],
        scratch_shapes=[pltpu.VMEM((ts, D), BF16), qk_scratch, qk_scratch, state],
        compiler_params=cparams,
    )(log_gamma, x, norm_w.reshape(1, D), w_in, w_in, cos, sin)

    rev = lambda b, t: (b, nt - 1 - t, 0)
    return pl.pallas_call(
        functools.partial(_ret_bwd_kernel, ts=ts),
        name="ret_bwd",
        grid=(B, nt),
        in_specs=[
            smem,
            pl.BlockSpec((1, ts, D), rev),
            _const_spec((1, D)),
            w_cols(1), w_cols(3),
            pl.BlockSpec((ts, half), lambda b, t: (nt - 1 - t, 0)),
            pl.BlockSpec((ts, half), lambda b, t: (nt - 1 - t, 0)),
            pl.BlockSpec((1, ts, RET_V), rev),
            pl.BlockSpec((1, ts, RET_V), rev),
            _const_spec((RET_V, D)),
        ],
        out_specs=pl.BlockSpec((1, ts, D), rev),
        out_shape=jax.ShapeDtypeStruct((B, S, D), F32),
        scratch_shapes=[pltpu.VMEM((ts, D), BF16), qk_scratch, qk_scratch,
                        pltpu.VMEM((ts, RET_V), BF16), state],
        compiler_params=cparams,
    )(log_gamma, x, norm_w.reshape(1, D), w_in, w_in, cos, sin, fwd, v, w_out.astype(BF16))


def kernel(x, pool_norm, pool_w_in, pool_w_group, pool_scale, pool_w_out,
           att_norm, att_w_in, att_q_norm, att_k_norm, att_w_out,
           ret_norm, ret_w_in, ret_decay, ret_w_out):
    depth = pool_norm.shape[0] + att_norm.shape[0] + ret_norm.shape[0]
    for layer in range(depth):
        kind, idx = layer % 3, layer // 3
        if kind == 0:
            x = _pool_layer(x, pool_norm[idx], pool_w_in[idx], pool_w_group[idx], pool_scale[idx],
                            pool_w_out[idx])
        elif kind == 1:
            x = _att_layer(x, att_norm[idx], att_w_in[idx], att_q_norm[idx], att_k_norm[idx],
                           att_w_out[idx])
        else:
            x = _ret_layer(x, ret_norm[idx], ret_w_in[idx], ret_decay[idx], ret_w_out[idx])
    return x
```

```python
import functools
import math

import jax
import jax.numpy as jnp
import numpy as np
from jax import lax
from jax.experimental import pallas as pl
from jax.experimental.pallas import tpu as pltpu

F32 = jnp.float32
BF16 = jnp.bfloat16

D_MODEL = 1024
EPS = 1e-6
NEG_BIG = -1e30
LOG2E = math.log2(math.e)
LN2 = math.log(2.0)

POOL_WINDOWS = (2, 4, 8, 16)
POOL_WIDTH = 2 * D_MODEL
POOL_GROUP = POOL_WIDTH // len(POOL_WINDOWS)

ATT_GROUPS = ((128, 1), (512, 4), (2048, 16))
ATT_HEAD_DIM = 128
ATT_HEADS = D_MODEL // ATT_HEAD_DIM
ATT_WIDTH = ATT_HEADS * ATT_HEAD_DIM
ATT_SIDE = 64
ROPE_DIM = ATT_HEAD_DIM // 4
ROPE_THETA = 500000.0

RET_HEADS = 4
RET_QK_DIM = D_MODEL // RET_HEADS
RET_V_DIM = 2 * D_MODEL // RET_HEADS
RET_QK = RET_HEADS * RET_QK_DIM
RET_V = RET_HEADS * RET_V_DIM
RET_CHUNK = 256
RET_THETA = 10000.0

V7X_VMEM_BYTES = 64 * 1024 * 1024
VMEM_LIMIT_BYTES = V7X_VMEM_BYTES - 8 * 1024 * 1024
LANES = 128
BF16_ROWS = 16

POOL_TILE = 512
POOL_HALO = BF16_ROWS
POOL_BLOCK = 128
ATT_PROJ_TILE = 1024
ATT_TILE = 2048
ATT_QBLOCK = 128
ATT_OUT_TILE = 1024
ATT_STRIDE_STEP = 4
ATT_MERGE_ROWS = 32
ATT_PAIRS = ATT_HEADS // 2
ATT_PAIR_W = 2 * ATT_HEAD_DIM
RET_TILE = 512


def _const_spec(shape):
    nd = len(shape)
    return pl.BlockSpec(shape, lambda *_: (0,) * nd, pipeline_mode=pl.Buffered(1))


def _rms(v, w):
    return v * lax.rsqrt(jnp.mean(v * v, axis=-1, keepdims=True) + EPS) * w


def _silu(g):
    return g * (1.0 / (1.0 + jnp.exp(-g)))


def _rope_angles(S, half, rot_dim, theta):
    inv_freq = (np.float32(1.0) / np.power(np.float32(theta),
                                           np.arange(half, dtype=np.float32) * np.float32(2.0 / rot_dim)))
    ang = np.arange(S, dtype=np.float32)[:, None] * inv_freq[None, :]
    return np.cos(ang).astype(np.float32), np.sin(ang).astype(np.float32)


def _dot(a, b):
    return jnp.dot(a, b, preferred_element_type=F32)


def _dot_nt(a, b):
    return lax.dot_general(a, b, (((1,), (1,)), ((), ())), preferred_element_type=F32)


def _dot_tn(a, b):
    return lax.dot_general(a, b, (((0,), (0,)), ((), ())), preferred_element_type=F32)


def _pool_kernel(x_ref, xp_ref, xn_ref, nw_ref, win_ref, wg_ref, sc_ref, wout_ref, band_ref,
                 o_ref, h_s, u_s, ub_s, d_s, y_s, *, ts, seq):
    t = pl.program_id(1)
    nt = pl.num_programs(1)
    nw = nw_ref[...]
    halo = POOL_HALO
    n_grp = len(POOL_WINDOWS)

    h_s[halo:halo + ts, :] = _rms(x_ref[0], nw).astype(BF16)
    hp = _rms(xp_ref[0], nw)
    h_s[0:halo, :] = jnp.where(t > 0, hp, 0.0).astype(BF16)
    hn = _rms(xn_ref[0], nw)
    h_s[halo + ts:, :] = jnp.where(t < nt - 1, hn, 0.0).astype(BF16)

    row = lax.broadcasted_iota(jnp.int32, (POOL_BLOCK, 1), 0)

    def project(g):
        ug = _dot(h_s[...], win_ref[:, g * POOL_GROUP:(g + 1) * POOL_GROUP])
        u_s[g] = ug
        ub_s[g] = ug.astype(BF16)

    def pool(g):
        window = POOL_WINDOWS[g]
        for b in range(ts // POOL_BLOCK):
            r0 = b * POOL_BLOCK
            wsum = _dot(band_ref[g], ub_s[g, r0:r0 + POOL_BLOCK + 2 * halo, :])
            pos = t * ts + r0 + row
            lo = jnp.maximum(pos - window // 2, 0)
            hi = jnp.minimum(pos - window // 2 + window, seq)
            diff = wsum / (hi - lo).astype(F32) - u_s[g, halo + r0:halo + r0 + POOL_BLOCK, :]
            d_s[g, r0:r0 + POOL_BLOCK, :] = diff.astype(BF16)

    def mix(g):
        cols = slice(g * POOL_GROUP, (g + 1) * POOL_GROUP)
        gate = _dot(h_s[halo:halo + ts, :], win_ref[:, POOL_WIDTH + g * POOL_GROUP:
                                                    POOL_WIDTH + (g + 1) * POOL_GROUP])
        mixed = _dot(d_s[g], wg_ref[g]) * sc_ref[:, cols]
        y_s[:, cols] = (mixed * _silu(gate)).astype(BF16)

    for step in range(n_grp + 2):
        if step < n_grp:
            project(step)
        if 0 <= step - 1 < n_grp:
            pool(step - 1)
        if 0 <= step - 2 < n_grp:
            mix(step - 2)
    o_ref[0] = x_ref[0] + _dot(y_s[...], wout_ref[...])


def _pool_band():
    i = np.arange(POOL_BLOCK)[:, None]
    j = np.arange(POOL_BLOCK + 2 * POOL_HALO)[None, :] - POOL_HALO
    bands = [((j - i >= -(w // 2)) & (j - i < w - w // 2)) for w in POOL_WINDOWS]
    return jnp.asarray(np.stack(bands).astype(np.float32), dtype=BF16)


def _pool_layer(x, norm_w, w_in, w_group, scale, w_out):
    B, S, D = x.shape
    ts, halo = POOL_TILE, POOL_HALO
    nt = S // ts
    hb = ts // halo
    n_hb = S // halo
    kern = functools.partial(_pool_kernel, ts=ts, seq=S)
    return pl.pallas_call(
        kern,
        name="pool_layer",
        grid=(B, nt),
        in_specs=[
            pl.BlockSpec((1, ts, D), lambda b, t: (b, t, 0)),
            pl.BlockSpec((1, halo, D), lambda b, t: (b, jnp.maximum(t * hb - 1, 0), 0)),
            pl.BlockSpec((1, halo, D), lambda b, t: (b, jnp.minimum((t + 1) * hb, n_hb - 1), 0)),
            _const_spec((1, D)),
            _const_spec((D, 2 * POOL_WIDTH)),
            _const_spec((len(POOL_WINDOWS), POOL_GROUP, POOL_GROUP)),
            _const_spec((1, POOL_WIDTH)),
            _const_spec((POOL_WIDTH, D)),
            _const_spec((len(POOL_WINDOWS), POOL_BLOCK, POOL_BLOCK + 2 * halo)),
        ],
        out_specs=pl.BlockSpec((1, ts, D), lambda b, t: (b, t, 0)),
        out_shape=jax.ShapeDtypeStruct((B, S, D), F32),
        scratch_shapes=[
            pltpu.VMEM((ts + 2 * halo, D), BF16),
            pltpu.VMEM((len(POOL_WINDOWS), ts + 2 * halo, POOL_GROUP), F32),
            pltpu.VMEM((len(POOL_WINDOWS), ts + 2 * halo, POOL_GROUP), BF16),
            pltpu.VMEM((len(POOL_WINDOWS), ts, POOL_GROUP), BF16),
            pltpu.VMEM((ts, POOL_WIDTH), BF16),
        ],
        compiler_params=pltpu.CompilerParams(
            dimension_semantics=("parallel", "parallel"), vmem_limit_bytes=VMEM_LIMIT_BYTES),
    )(x, x, x, norm_w.reshape(1, D), w_in.astype(BF16), w_group.astype(BF16),
      scale.reshape(1, POOL_WIDTH), w_out.astype(BF16), _pool_band())


def _att_head_perm():
    half = ROPE_DIM // 2
    rest = (ATT_HEAD_DIM - ROPE_DIM) // 2
    order = (list(range(half)) + list(range(ROPE_DIM, ROPE_DIM + rest))
             + list(range(half, ROPE_DIM)) + list(range(ROPE_DIM + rest, ATT_HEAD_DIM)))
    return np.array(order, dtype=np.int32)


def _att_rope_tables(S):
    half = ROPE_DIM // 2
    cos, sin = _rope_angles(S, half, ROPE_DIM, ROPE_THETA)
    pad = ATT_HEAD_DIM // 2 - half
    one, zero = np.ones((S, pad), np.float32), np.zeros((S, pad), np.float32)
    return (np.concatenate([cos, one, cos, one], axis=1),
            np.concatenate([-sin, zero, sin, zero], axis=1))


def _att_proj_kernel(x_ref, nw_ref, w_ref, qn_ref, kn_ref, hsum_ref, cos_ref, sin_ref,
                     q_ref, k_ref, v_ref, h_s, hf_s, hg_s, *, ts, dil):
    hd = ATT_HEAD_DIM
    n = ts // dil
    hn = _rms(x_ref[0], nw_ref[...])
    if dil == 1:
        h_s[...] = hn.astype(BF16)
    else:
        n_slab = hn.shape[1] // LANES
        for c in range(n_slab):
            hf_s[c] = hn[:, c * LANES:(c + 1) * LANES]
        if dil > ATT_STRIDE_STEP:
            step = ATT_STRIDE_STEP
            for c in range(n_slab):
                for lo in range(step):
                    hg_s[c, lo] = hf_s[c, pl.ds(lo, ts // step, stride=step), :]
            for r in range(dil):
                hi, lo = divmod(r, step)
                rows = pl.ds(hi, n, stride=dil // step)
                h_s[r * n:(r + 1) * n, :] = jnp.concatenate(
                    [hg_s[c, lo, rows, :] for c in range(n_slab)], axis=1).astype(BF16)
        else:
            for r in range(dil):
                rows = pl.ds(r, n, stride=dil)
                h_s[r * n:(r + 1) * n, :] = jnp.concatenate(
                    [hf_s[c, rows, :] for c in range(n_slab)], axis=1).astype(BF16)
    cos_t, sin_t = cos_ref[...], sin_ref[...]

    def emit(val, out_ref, pair, hh):
        out_ref[0, pair, :, :, hh * hd:(hh + 1) * hd] = val.astype(BF16).reshape(dil, n, hd)

    def qk_head(val, inv_rms, gain):
        y = val * inv_rms * gain
        return y * cos_t + pltpu.roll(y, hd // 2, 1) * sin_t

    gains = (qn_ref[...] * (hd ** -0.5 * LOG2E), kn_ref[...])
    out_refs = (q_ref, k_ref, v_ref)
    n_pairs = 3 * ATT_PAIRS

    def project(i):
        return _dot(h_s[...], w_ref[:, i * ATT_PAIR_W:(i + 1) * ATT_PAIR_W])

    nxt = project(0)
    for i in range(n_pairs):
        proj = nxt
        if i + 1 < n_pairs:
            nxt = project(i + 1)
        kind, pair = divmod(i, ATT_PAIRS)
        if kind < 2:
            ssq = _dot((proj * proj).astype(BF16), hsum_ref[...])
            inv_rms = lax.rsqrt(ssq * (1.0 / hd) + EPS)
        for hh in range(2):
            lanes = slice(hh * hd, (hh + 1) * hd)
            if kind < 2:
                emit(qk_head(proj[:, lanes], inv_rms[:, lanes], gains[kind]), out_refs[kind], pair, hh)
            else:
                emit(proj[:, lanes], v_ref, pair, hh)


def _att_proj(x, norm_w, w_in, group, q_gain, k_gain, tables, dil):
    B, S, D = x.shape
    ts = ATT_PROJ_TILE
    n = ts // dil
    kern = functools.partial(_att_proj_kernel, ts=ts, dil=dil)
    tables = [tab.reshape(S // ts, n, dil, ATT_HEAD_DIM).transpose(0, 2, 1, 3).reshape(S, ATT_HEAD_DIM)
              for tab in tables]
    tab_spec = pl.BlockSpec((ts, ATT_HEAD_DIM), lambda b, t: (t, 0))
    out_spec = pl.BlockSpec((1, ATT_PAIRS, dil, n, ATT_PAIR_W), lambda b, t: (b, 0, 0, t, 0))
    out_sds = jax.ShapeDtypeStruct((B, ATT_PAIRS, dil, S // dil, ATT_PAIR_W), BF16)
    head_sum = jnp.asarray(np.kron(np.eye(2, dtype=np.float32),
                                   np.ones((ATT_HEAD_DIM, ATT_HEAD_DIM), np.float32)), dtype=BF16)
    return pl.pallas_call(
        kern,
        name=f"att_proj_d{dil}",
        grid=(B, S // ts),
        in_specs=[
            pl.BlockSpec((1, ts, D), lambda b, t: (b, t, 0)),
            _const_spec((1, D)),
            pl.BlockSpec((D, 3 * ATT_WIDTH), lambda b, t: (0, group), pipeline_mode=pl.Buffered(1)),
            _const_spec((1, ATT_HEAD_DIM)),
            _const_spec((1, ATT_HEAD_DIM)),
            _const_spec((ATT_PAIR_W, ATT_PAIR_W)),
            tab_spec, tab_spec,
        ],
        out_specs=[out_spec, out_spec, out_spec],
        out_shape=[out_sds, out_sds, out_sds],
        scratch_shapes=[pltpu.VMEM((ts, D), BF16), pltpu.VMEM((D // LANES, ts, LANES), F32),
                        pltpu.VMEM((D // LANES, ATT_STRIDE_STEP, ts // ATT_STRIDE_STEP, LANES), F32)],
        compiler_params=pltpu.CompilerParams(
            dimension_semantics=("parallel", "parallel"), vmem_limit_bytes=VMEM_LIMIT_BYTES),
    )(x, norm_w.reshape(1, D), w_in, q_gain.reshape(1, -1), k_gain.reshape(1, -1), head_sum,
      *tables)


def _att_core_kernel(*refs, ts, seq):
    n_g = len(ATT_GROUPS)
    grp = [refs[7 * g:7 * g + 7] for g in range(n_g)]
    o_ref = refs[7 * n_g]
    og_s, lse_s = refs[7 * n_g + 1:]
    t = pl.program_id(1)
    side = ATT_SIDE
    hd = ATT_HEAD_DIM
    qb = ATT_QBLOCK
    nk = qb + 2 * side

    for g, (_, dil) in enumerate(ATT_GROUPS):
        q_ref, km_ref, kp_ref, kn_ref, vm_ref, vp_ref, vn_ref = grp[g]
        n = ts // dil
        n_sub = seq // dil

        def window(main_ref, prev_ref, next_ref, r, c0):
            parts = []
            if c0 == 0:
                parts.append(prev_ref[0, 0, r])
                lo = 0
            else:
                lo = c0 - side
            hi = min(c0 + qb + side, n)
            parts.append(main_ref[0, 0, r, lo:hi, :])
            if c0 + qb + side > n:
                parts.append(next_ref[0, 0, r])
            return parts[0] if len(parts) == 1 else jnp.concatenate(parts, axis=0)

        @pl.when(t >= 0)
        def _():
            a = lax.broadcasted_iota(jnp.int32, (qb, nk), 0)
            j = lax.broadcasted_iota(jnp.int32, (qb, nk), 1)
            band_bias = jnp.where((j - a >= 0) & (j - a <= 2 * side), 0.0, NEG_BIG).astype(F32)
            jrow = lax.broadcasted_iota(jnp.int32, (1, nk), 1)
            ones = jnp.ones((nk, hd), BF16)
            for r in range(dil):
                for c0 in range(0, n, qb):
                    base = t * n + c0 - side
                    key_ok = (jrow + base >= 0) & (jrow + base < n_sub)
                    bias = band_bias + jnp.where(key_ok, 0.0, NEG_BIG).astype(F32)
                    qv = q_ref[0, 0, r, c0:c0 + qb, :]
                    kv = window(km_ref, kp_ref, kn_ref, r, c0)
                    vv = window(vm_ref, vp_ref, vn_ref, r, c0)
                    for hh in range(2):
                        cols = slice(hh * hd, (hh + 1) * hd)
                        s = _dot_nt(qv[:, cols], kv[:, cols]) + bias
                        m = jnp.max(s, axis=-1, keepdims=True)
                        p = jnp.exp2(s - m).astype(BF16)
                        acc = _dot(p, jnp.concatenate([vv[:, cols], ones], axis=1))
                        den = acc[:, hd:]
                        o = acc[:, :hd] / den
                        lse = m * LN2 + jnp.log(den)
                        if dil == 1:
                            rows = slice(c0, c0 + qb)
                        else:
                            rows = pl.ds(c0 * dil + r, qb, stride=dil)
                        og_s[g, hh, rows, :] = o
                        lse_s[g, hh, rows, :] = lse

    for c0 in range(0, ts, ATT_MERGE_ROWS):
        rows = slice(c0, c0 + ATT_MERGE_ROWS)
        ys = []
        for hh in range(2):
            lses = [lse_s[g, hh, rows, :] for g in range(n_g)]
            m = functools.reduce(jnp.maximum, lses)
            ws = [jnp.exp(l - m) for l in lses]
            den = functools.reduce(lambda u, v: u + v, ws)
            num = functools.reduce(lambda u, v: u + v,
                                   [w * og_s[g, hh, rows, :] for g, w in enumerate(ws)])
            ys.append(num / den)
        o_ref[0, rows, :] = jnp.concatenate(ys, axis=1).astype(BF16)


def _att_core(qkv, S):
    B = qkv[0][0].shape[0]
    ts = ATT_TILE
    side = ATT_SIDE
    pw = ATT_PAIR_W
    in_specs, args = [], []
    for (_, dil), (q, k, v) in zip(ATT_GROUPS, qkv):
        n = ts // dil
        per = n // side
        last = S // dil // side - 1
        main = pl.BlockSpec((1, 1, dil, n, pw), lambda b, t, h: (b, h, 0, t, 0))
        prev = pl.BlockSpec((1, 1, dil, side, pw),
                            lambda b, t, h, per=per: (b, h, 0, jnp.maximum(t * per - 1, 0), 0))
        nxt = pl.BlockSpec((1, 1, dil, side, pw),
                           lambda b, t, h, per=per, last=last: (b, h, 0, jnp.minimum((t + 1) * per, last), 0))
        in_specs += [main, main, prev, nxt, main, prev, nxt]
        args += [q, k, k, k, v, v, v]
    kern = functools.partial(_att_core_kernel, ts=ts, seq=S)
    n_g = len(ATT_GROUPS)
    return pl.pallas_call(
        kern,
        name="att_core",
        grid=(B, S // ts, ATT_PAIRS),
        in_specs=in_specs,
        out_specs=pl.BlockSpec((1, ts, pw), lambda b, t, h: (b, t, h)),
        out_shape=jax.ShapeDtypeStruct((B, S, ATT_WIDTH), BF16),
        scratch_shapes=[
            pltpu.VMEM((n_g, 2, ts, ATT_HEAD_DIM), F32),
            pltpu.VMEM((n_g, 2, ts, ATT_HEAD_DIM), F32),
        ],
        compiler_params=pltpu.CompilerParams(
            dimension_semantics=("parallel", "parallel", "parallel"),
            vmem_limit_bytes=VMEM_LIMIT_BYTES),
    )(*args)


def _att_out_kernel(x_ref, a_ref, nw_ref, wgate_ref, wout_ref, o_ref):
    h = _rms(x_ref[0], nw_ref[...]).astype(BF16)
    gate = _dot(h, wgate_ref[...])
    y = (a_ref[0].astype(F32) * _silu(gate)).astype(BF16)
    o_ref[0] = x_ref[0] + _dot(y, wout_ref[...])


def _att_out(x, att, norm_w, w_in, gate_block, w_out):
    B, S, D = x.shape
    ts = ATT_OUT_TILE
    return pl.pallas_call(
        _att_out_kernel,
        name="att_out",
        grid=(B, S // ts),
        in_specs=[
            pl.BlockSpec((1, ts, D), lambda b, t: (b, t, 0)),
            pl.BlockSpec((1, ts, ATT_WIDTH), lambda b, t: (b, t, 0)),
            _const_spec((1, D)),
            pl.BlockSpec((D, ATT_WIDTH), lambda b, t: (0, gate_block), pipeline_mode=pl.Buffered(1)),
            _const_spec((ATT_WIDTH, D)),
        ],
        out_specs=pl.BlockSpec((1, ts, D), lambda b, t: (b, t, 0)),
        out_shape=jax.ShapeDtypeStruct((B, S, D), F32),
        compiler_params=pltpu.CompilerParams(
            dimension_semantics=("parallel", "parallel"), vmem_limit_bytes=VMEM_LIMIT_BYTES),
    )(x, att, norm_w.reshape(1, D), w_in, w_out)


def _att_layer(x, norm_w, w_in, q_norm, k_norm, w_out):
    B, S, D = x.shape
    n_g = len(ATT_GROUPS)
    tables = _att_rope_tables(S)
    perm = _att_head_perm()
    head = np.arange(w_in.shape[1] // ATT_HEAD_DIM)
    is_qk = ((head // ATT_HEADS) % 3 < 2) & (head < 3 * n_g * ATT_HEADS)
    w_heads = w_in.reshape(D, -1, ATT_HEAD_DIM)
    cuts = [0, ROPE_DIM // 2, ROPE_DIM, ROPE_DIM + (ATT_HEAD_DIM - ROPE_DIM) // 2, ATT_HEAD_DIM]
    parts = [w_heads[..., lo:hi] for lo, hi in zip(cuts[:-1], cuts[1:])]
    w_perm = jnp.concatenate([parts[0], parts[2], parts[1], parts[3]], axis=-1)
    w_in = jnp.where(is_qk[None, :, None], w_perm, w_heads).astype(BF16).reshape(D, -1)
    q_norm, k_norm = q_norm[:, perm], k_norm[:, perm]
    qkv = []
    for g, (_, dil) in enumerate(ATT_GROUPS):
        qkv.append(_att_proj(x, norm_w, w_in, g, q_norm[g], k_norm[g], tables, dil))
    att = _att_core(qkv, S)
    return _att_out(x, att, norm_w, w_in, 3 * n_g, w_out.astype(BF16))


def _ret_rope_tables(S):
    return _rope_angles(S, RET_QK_DIM // 2, RET_QK_DIM, RET_THETA)


def _ret_rotate(val, cos, sin, scale):
    half = RET_QK_DIM // 2
    x1, x2 = val[:, :half], val[:, half:]
    out = jnp.concatenate([x1 * cos - x2 * sin, x2 * cos + x1 * sin], axis=1)
    return out * scale if scale != 1.0 else out


def _ret_decays(lg, reverse):
    C = RET_CHUNK
    i = lax.broadcasted_iota(jnp.int32, (C, C), 0)
    jj = lax.broadcasted_iota(jnp.int32, (C, C), 1)
    col = lax.broadcasted_iota(jnp.int32, (C, 1), 0).astype(F32)
    if reverse:
        dist, keep = jj - i, jj > i
        q_dec = jnp.exp((C - col) * lg)
        k_dec = jnp.exp(col * lg)
    else:
        dist, keep = i - jj, i >= jj
        q_dec = jnp.exp((col + 1.0) * lg)
        k_dec = jnp.exp((C - 1.0 - col) * lg)
    decay = jnp.where(keep, jnp.exp(jnp.where(keep, dist, 0).astype(F32) * lg), 0.0)
    chunk_dec = jnp.exp(jnp.full((1, 1), float(C), F32) * lg)
    return decay, q_dec, k_dec, chunk_dec


def _ret_project_qk(h_s, w_ref, cos_ref, sin_ref, q_s, k_s):
    cos, sin = cos_ref[...], sin_ref[...]
    for hd in range(RET_HEADS):
        qc = slice(hd * RET_QK_DIM, (hd + 1) * RET_QK_DIM)
        kc = slice(RET_QK + hd * RET_QK_DIM, RET_QK + (hd + 1) * RET_QK_DIM)
        q_s[hd] = _ret_rotate(_dot(h_s[...], w_ref[:, qc]), cos, sin, 1.0)
        k_s[hd] = _ret_rotate(_dot(h_s[...], w_ref[:, kc]), cos, sin, RET_QK_DIM ** -0.5)


def _ret_scan_chunk(rows, lg_row, q_s, k_s, v_at, st_s, reverse, filler=None):
    heads = range(RET_HEADS)
    decs = [_ret_decays(lg_row(hd), reverse) for hd in heads]
    q = [q_s[hd, rows, :] for hd in heads]
    k = [k_s[hd, rows, :] for hd in heads]
    v = [v_at(hd) for hd in heads]
    inner = [_dot_nt(q[hd].astype(BF16), k[hd].astype(BF16)) for hd in heads]
    kv = [_dot_tn((k[hd] * decs[hd][2]).astype(BF16), v[hd]) for hd in heads]
    extra = filler() if filler is not None else None
    outs = []
    for hd in heads:
        decay, q_dec, _, chunk_dec = decs[hd]
        state = st_s[hd]
        outs.append(_dot((inner[hd] * decay).astype(BF16), v[hd])
                    + _dot((q[hd] * q_dec).astype(BF16), state.astype(BF16)))
        st_s[hd] = state * chunk_dec + kv[hd]
    return outs, extra


def _ret_fwd_kernel(lg_ref, x_ref, nw_ref, w_ref, wv_ref, cos_ref, sin_ref, f_ref, v_ref,
                    h_s, q_s, k_s, st_s, *, ts):
    @pl.when(pl.program_id(1) == 0)
    def _():
        st_s[...] = jnp.zeros_like(st_s)

    h_s[...] = _rms(x_ref[0], nw_ref[...]).astype(BF16)
    _ret_project_qk(h_s, w_ref, cos_ref, sin_ref, q_s, k_s)
    for hd in range(RET_HEADS):
        vc = slice(hd * RET_V_DIM, (hd + 1) * RET_V_DIM)
        v_ref[0, :, vc] = _dot(h_s[...], wv_ref[:, vc]).astype(BF16)
    for c in range(ts // RET_CHUNK):
        rows = slice(c * RET_CHUNK, (c + 1) * RET_CHUNK)
        outs, _ = _ret_scan_chunk(
            rows, lambda hd: lg_ref[0, hd], q_s, k_s,
            lambda hd: v_ref[0, rows, hd * RET_V_DIM:(hd + 1) * RET_V_DIM], st_s, False)
        for hd in range(RET_HEADS):
            f_ref[0, rows, hd * RET_V_DIM:(hd + 1) * RET_V_DIM] = outs[hd]


def _ret_bwd_kernel(lg_ref, x_ref, nw_ref, w_ref, wg_ref, cos_ref, sin_ref, f_ref, v_ref, wout_ref,
                    o_ref, h_s, q_s, k_s, y_s, st_s, *, ts):
    @pl.when(pl.program_id(1) == 0)
    def _():
        st_s[...] = jnp.zeros_like(st_s)

    h_s[...] = _rms(x_ref[0], nw_ref[...]).astype(BF16)
    _ret_project_qk(h_s, w_ref, cos_ref, sin_ref, q_s, k_s)
    for c in reversed(range(ts // RET_CHUNK)):
        rows = slice(c * RET_CHUNK, (c + 1) * RET_CHUNK)

        def gates():
            return [_dot(h_s[rows, :], wg_ref[:, hd * RET_V_DIM:(hd + 1) * RET_V_DIM])
                    for hd in range(RET_HEADS)]

        outs, gate = _ret_scan_chunk(
            rows, lambda hd: lg_ref[1, hd], q_s, k_s,
            lambda hd: v_ref[0, rows, hd * RET_V_DIM:(hd + 1) * RET_V_DIM], st_s, True, gates)
        for hd in range(RET_HEADS):
            oc = slice(hd * RET_V_DIM, (hd + 1) * RET_V_DIM)
            y = outs[hd] + f_ref[0, rows, oc]
            y = y * lax.rsqrt(jnp.mean(y * y, axis=-1, keepdims=True) + EPS)
            y_s[rows, oc] = (y * _silu(gate[hd])).astype(BF16)
    o_ref[0] = x_ref[0] + _dot(y_s[...], wout_ref[...])


def _ret_layer(x, norm_w, w_in, decay_exp, w_out):
    B, S, D = x.shape
    ts = RET_TILE
    nt = S // ts
    w_in = w_in.astype(BF16)

    def w_cols(j):
        return pl.BlockSpec((D, 2 * RET_QK), lambda b, t: (0, j), pipeline_mode=pl.Buffered(1))

    log_gamma = jnp.log1p(-jnp.exp2(-decay_exp.astype(F32)))
    cos, sin = _ret_rope_tables(S)
    half = RET_QK_DIM // 2
    smem = pl.BlockSpec(memory_space=pltpu.SMEM)
    cparams = pltpu.CompilerParams(
        dimension_semantics=("parallel", "arbitrary"), vmem_limit_bytes=VMEM_LIMIT_BYTES)
    state = pltpu.VMEM((RET_HEADS, RET_QK_DIM, RET_V_DIM), F32)
    qk_scratch = pltpu.VMEM((RET_HEADS, ts, RET_QK_DIM), F32)

    fwd, v = pl.pallas_call(
        functools.partial(_ret_fwd_kernel, ts=ts),
        name="ret_fwd",
        grid=(B, nt),
        in_specs=[
            smem,
            pl.BlockSpec((1, ts, D), lambda b, t: (b, t, 0)),
            _const_spec((1, D)),
            w_cols(0), w_cols(2),
            pl.BlockSpec((ts, half), lambda b, t: (t, 0)),
            pl.BlockSpec((ts, half), lambda b, t: (t, 0)),
        ],
        out_specs=[pl.BlockSpec((1, ts, RET_V), lambda b, t: (b, t, 0)),
                   pl.BlockSpec((1, ts, RET_V), lambda b, t: (b, t, 0))],
        out_shape=[jax.ShapeDtypeStruct((B, S, RET_V), F32),
                   jax.ShapeDtypeStruct((B, S, RET_V), BF16)],
        scratch_shapes=[pltpu.VMEM((ts, D), BF16), qk_scratch, qk_scratch, state],
        compiler_params=cparams,
    )(log_gamma, x, norm_w.reshape(1, D), w_in, w_in, cos, sin)

    rev = lambda b, t: (b, nt - 1 - t, 0)
    return pl.pallas_call(
        functools.partial(_ret_bwd_kernel, ts=ts),
        name="ret_bwd",
        grid=(B, nt),
        in_specs=[
            smem,
            pl.BlockSpec((1, ts, D), rev),
            _const_spec((1, D)),
            w_cols(1), w_cols(3),
            pl.BlockSpec((ts, half), lambda b, t: (nt - 1 - t, 0)),
            pl.BlockSpec((ts, half), lambda b, t: (nt - 1 - t, 0)),
            pl.BlockSpec((1, ts, RET_V), rev),
            pl.BlockSpec((1, ts, RET_V), rev),
            _const_spec((RET_V, D)),
        ],
        out_specs=pl.BlockSpec((1, ts, D), rev),
        out_shape=jax.ShapeDtypeStruct((B, S, D), F32),
        scratch_shapes=[pltpu.VMEM((ts, D), BF16), qk_scratch, qk_scratch,
                        pltpu.VMEM((ts, RET_V), BF16), state],
        compiler_params=cparams,
    )(log_gamma, x, norm_w.reshape(1, D), w_in, w_in, cos, sin, fwd, v, w_out.astype(BF16))


def kernel(x, pool_norm, pool_w_in, pool_w_group, pool_scale, pool_w_out,
           att_norm, att_w_in, att_q_norm, att_k_norm, att_w_out,
           ret_norm, ret_w_in, ret_decay, ret_w_out):
    depth = pool_norm.shape[0] + att_norm.shape[0] + ret_norm.shape[0]
    for layer in range(depth):
        kind, idx = layer % 3, layer // 3
        if kind == 0:
            x = _pool_layer(x, pool_norm[idx], pool_w_in[idx], pool_w_group[idx], pool_scale[idx],
                            pool_w_out[idx])
        elif kind == 1:
            x = _att_layer(x, att_norm[idx], att_w_in[idx], att_q_norm[idx], att_k_norm[idx],
                           att_w_out[idx])
        else:
            x = _ret_layer(x, ret_norm[idx], ret_w_in[idx], ret_decay[idx], ret_w_out[idx])
    return x
```

```python
import functools
import math

import jax
import jax.numpy as jnp
import numpy as np
from jax import lax
from jax.experimental import pallas as pl
from jax.experimental.pallas import tpu as pltpu

F32 = jnp.float32
BF16 = jnp.bfloat16

D_MODEL = 1024
EPS = 1e-6
NEG_BIG = -1e30
LOG2E = math.log2(math.e)
LN2 = math.log(2.0)

POOL_WINDOWS = (2, 4, 8, 16)
POOL_WIDTH = 2 * D_MODEL
POOL_GROUP = POOL_WIDTH // len(POOL_WINDOWS)

ATT_GROUPS = ((128, 1), (512, 4), (2048, 16))
ATT_HEAD_DIM = 128
ATT_HEADS = D_MODEL // ATT_HEAD_DIM
ATT_WIDTH = ATT_HEADS * ATT_HEAD_DIM
ATT_SIDE = 64
ROPE_DIM = ATT_HEAD_DIM // 4
ROPE_THETA = 500000.0

RET_HEADS = 4
RET_QK_DIM = D_MODEL // RET_HEADS
RET_V_DIM = 2 * D_MODEL // RET_HEADS
RET_QK = RET_HEADS * RET_QK_DIM
RET_V = RET_HEADS * RET_V_DIM
RET_CHUNK = 256
RET_THETA = 10000.0

V7X_VMEM_BYTES = 64 * 1024 * 1024
VMEM_LIMIT_BYTES = V7X_VMEM_BYTES - 8 * 1024 * 1024
LANES = 128
BF16_ROWS = 16

POOL_TILE = 512
POOL_HALO = BF16_ROWS
POOL_BLOCK = 128
ATT_PROJ_TILE = 1024
ATT_TILE = 2048
ATT_QBLOCK = 128
ATT_OUT_TILE = 1024
ATT_STRIDE_STEP = 4
ATT_MERGE_ROWS = 32
ATT_PAIRS = ATT_HEADS // 2
ATT_PAIR_W = 2 * ATT_HEAD_DIM
RET_TILE = 512


def _const_spec(shape):
    nd = len(shape)
    return pl.BlockSpec(shape, lambda *_: (0,) * nd, pipeline_mode=pl.Buffered(1))


def _rms(v, w):
    return v * lax.rsqrt(jnp.mean(v * v, axis=-1, keepdims=True) + EPS) * w


def _silu(g):
    return g * (1.0 / (1.0 + jnp.exp(-g)))


def _rope_angles(S, half, rot_dim, theta):
    inv_freq = (np.float32(1.0) / np.power(np.float32(theta),
                                           np.arange(half, dtype=np.float32) * np.float32(2.0 / rot_dim)))
    ang = np.arange(S, dtype=np.float32)[:, None] * inv_freq[None, :]
    return np.cos(ang).astype(np.float32), np.sin(ang).astype(np.float32)


def _dot(a, b):
    return jnp.dot(a, b, preferred_element_type=F32)


def _dot_nt(a, b):
    return lax.dot_general(a, b, (((1,), (1,)), ((), ())), preferred_element_type=F32)


def _dot_tn(a, b):
    return lax.dot_general(a, b, (((0,), (0,)), ((), ())), preferred_element_type=F32)


def _pool_kernel(x_ref, xp_ref, xn_ref, nw_ref, win_ref, wg_ref, sc_ref, wout_ref, band_ref,
                 o_ref, h_s, u_s, ub_s, d_s, y_s, *, ts, seq):
    t = pl.program_id(1)
    nt = pl.num_programs(1)
    nw = nw_ref[...]
    halo = POOL_HALO
    n_grp = len(POOL_WINDOWS)

    h_s[halo:halo + ts, :] = _rms(x_ref[0], nw).astype(BF16)
    hp = _rms(xp_ref[0], nw)
    h_s[0:halo, :] = jnp.where(t > 0, hp, 0.0).astype(BF16)
    hn = _rms(xn_ref[0], nw)
    h_s[halo + ts:, :] = jnp.where(t < nt - 1, hn, 0.0).astype(BF16)

    row = lax.broadcasted_iota(jnp.int32, (POOL_BLOCK, 1), 0)

    def project(g):
        ug = _dot(h_s[...], win_ref[:, g * POOL_GROUP:(g + 1) * POOL_GROUP])
        u_s[g] = ug
        ub_s[g] = ug.astype(BF16)

    def pool(g):
        window = POOL_WINDOWS[g]
        for b in range(ts // POOL_BLOCK):
            r0 = b * POOL_BLOCK
            wsum = _dot(band_ref[g], ub_s[g, r0:r0 + POOL_BLOCK + 2 * halo, :])
            pos = t * ts + r0 + row
            lo = jnp.maximum(pos - window // 2, 0)
            hi = jnp.minimum(pos - window // 2 + window, seq)
            diff = wsum / (hi - lo).astype(F32) - u_s[g, halo + r0:halo + r0 + POOL_BLOCK, :]
            d_s[g, r0:r0 + POOL_BLOCK, :] = diff.astype(BF16)

    def mix(g):
        cols = slice(g * POOL_GROUP, (g + 1) * POOL_GROUP)
        gate = _dot(h_s[halo:halo + ts, :], win_ref[:, POOL_WIDTH + g * POOL_GROUP:
                                                    POOL_WIDTH + (g + 1) * POOL_GROUP])
        mixed = _dot(d_s[g], wg_ref[g]) * sc_ref[:, cols]
        y_s[:, cols] = (mixed * _silu(gate)).astype(BF16)

    for step in range(n_grp + 2):
        if step < n_grp:
            project(step)
        if 0 <= step - 1 < n_grp:
            pool(step - 1)
        if 0 <= step - 2 < n_grp:
            mix(step - 2)
    o_ref[0] = x_ref[0] + _dot(y_s[...], wout_ref[...])


def _pool_band():
    i = np.arange(POOL_BLOCK)[:, None]
    j = np.arange(POOL_BLOCK + 2 * POOL_HALO)[None, :] - POOL_HALO
    bands = [((j - i >= -(w // 2)) & (j - i < w - w // 2)) for w in POOL_WINDOWS]
    return jnp.asarray(np.stack(bands).astype(np.float32), dtype=BF16)


def _pool_layer(x, norm_w, w_in, w_group, scale, w_out):
    B, S, D = x.shape
    ts, halo = POOL_TILE, POOL_HALO
    nt = S // ts
    hb = ts // halo
    n_hb = S // halo
    kern = functools.partial(_pool_kernel, ts=ts, seq=S)
    return pl.pallas_call(
        kern,
        name="pool_layer",
        grid=(B, nt),
        in_specs=[
            pl.BlockSpec((1, ts, D), lambda b, t: (b, t, 0)),
            pl.BlockSpec((1, halo, D), lambda b, t: (b, jnp.maximum(t * hb - 1, 0), 0)),
            pl.BlockSpec((1, halo, D), lambda b, t: (b, jnp.minimum((t + 1) * hb, n_hb - 1), 0)),
            _const_spec((1, D)),
            _const_spec((D, 2 * POOL_WIDTH)),
            _const_spec((len(POOL_WINDOWS), POOL_GROUP, POOL_GROUP)),
            _const_spec((1, POOL_WIDTH)),
            _const_spec((POOL_WIDTH, D)),
            _const_spec((len(POOL_WINDOWS), POOL_BLOCK, POOL_BLOCK + 2 * halo)),
        ],
        out_specs=pl.BlockSpec((1, ts, D), lambda b, t: (b, t, 0)),
        out_shape=jax.ShapeDtypeStruct((B, S, D), F32),
        scratch_shapes=[
            pltpu.VMEM((ts + 2 * halo, D), BF16),
            pltpu.VMEM((len(POOL_WINDOWS), ts + 2 * halo, POOL_GROUP), F32),
            pltpu.VMEM((len(POOL_WINDOWS), ts + 2 * halo, POOL_GROUP), BF16),
            pltpu.VMEM((len(POOL_WINDOWS), ts, POOL_GROUP), BF16),
            pltpu.VMEM((ts, POOL_WIDTH), BF16),
        ],
        compiler_params=pltpu.CompilerParams(
            dimension_semantics=("parallel", "parallel"), vmem_limit_bytes=VMEM_LIMIT_BYTES),
    )(x, x, x, norm_w.reshape(1, D), w_in.astype(BF16), w_group.astype(BF16),
      scale.reshape(1, POOL_WIDTH), w_out.astype(BF16), _pool_band())


def _att_head_perm():
    half = ROPE_DIM // 2
    rest = (ATT_HEAD_DIM - ROPE_DIM) // 2
    order = (list(range(half)) + list(range(ROPE_DIM, ROPE_DIM + rest))
             + list(range(half, ROPE_DIM)) + list(range(ROPE_DIM + rest, ATT_HEAD_DIM)))
    return np.array(order, dtype=np.int32)


def _att_rope_tables(S):
    half = ROPE_DIM // 2
    cos, sin = _rope_angles(S, half, ROPE_DIM, ROPE_THETA)
    pad = ATT_HEAD_DIM // 2 - half
    one, zero = np.ones((S, pad), np.float32), np.zeros((S, pad), np.float32)
    return (np.concatenate([cos, one, cos, one], axis=1),
            np.concatenate([-sin, zero, sin, zero], axis=1))


def _att_proj_kernel(x_ref, nw_ref, w_ref, qn_ref, kn_ref, hsum_ref, cos_ref, sin_ref,
                     q_ref, k_ref, v_ref, h_s, hf_s, hg_s, *, ts, dil):
    hd = ATT_HEAD_DIM
    n = ts // dil
    hn = _rms(x_ref[0], nw_ref[...])
    if dil == 1:
        h_s[...] = hn.astype(BF16)
    else:
        n_slab = hn.shape[1] // LANES
        for c in range(n_slab):
            hf_s[c] = hn[:, c * LANES:(c + 1) * LANES]
        if dil > ATT_STRIDE_STEP:
            step = ATT_STRIDE_STEP
            for c in range(n_slab):
                for lo in range(step):
                    hg_s[c, lo] = hf_s[c, pl.ds(lo, ts // step, stride=step), :]
            for r in range(dil):
                hi, lo = divmod(r, step)
                rows = pl.ds(hi, n, stride=dil // step)
                h_s[r * n:(r + 1) * n, :] = jnp.concatenate(
                    [hg_s[c, lo, rows, :] for c in range(n_slab)], axis=1).astype(BF16)
        else:
            for r in range(dil):
                rows = pl.ds(r, n, stride=dil)
                h_s[r * n:(r + 1) * n, :] = jnp.concatenate(
                    [hf_s[c, rows, :] for c in range(n_slab)], axis=1).astype(BF16)
    cos_t, sin_t = cos_ref[...], sin_ref[...]

    def emit(val, out_ref, pair, hh):
        out_ref[0, pair, :, :, hh * hd:(hh + 1) * hd] = val.astype(BF16).reshape(dil, n, hd)

    def qk_head(val, inv_rms, gain):
        y = val * inv_rms * gain
        return y * cos_t + pltpu.roll(y, hd // 2, 1) * sin_t

    gains = (qn_ref[...] * (hd ** -0.5 * LOG2E), kn_ref[...])
    out_refs = (q_ref, k_ref, v_ref)
    n_pairs = 3 * ATT_PAIRS

    def project(i):
        return _dot(h_s[...], w_ref[:, i * ATT_PAIR_W:(i + 1) * ATT_PAIR_W])

    nxt = project(0)
    for i in range(n_pairs):
        proj = nxt
        if i + 1 < n_pairs:
            nxt = project(i + 1)
        kind, pair = divmod(i, ATT_PAIRS)
        if kind < 2:
            ssq = _dot((proj * proj).astype(BF16), hsum_ref[...])
            inv_rms = lax.rsqrt(ssq * (1.0 / hd) + EPS)
        for hh in range(2):
            lanes = slice(hh * hd, (hh + 1) * hd)
            if kind < 2:
                emit(qk_head(proj[:, lanes], inv_rms[:, lanes], gains[kind]), out_refs[kind], pair, hh)
            else:
                emit(proj[:, lanes], v_ref, pair, hh)


def _att_proj(x, norm_w, w_in, group, q_gain, k_gain, tables, dil):
    B, S, D = x.shape
    ts = ATT_PROJ_TILE
    n = ts // dil
    kern = functools.partial(_att_proj_kernel, ts=ts, dil=dil)
    tables = [tab.reshape(S // ts, n, dil, ATT_HEAD_DIM).transpose(0, 2, 1, 3).reshape(S, ATT_HEAD_DIM)
              for tab in tables]
    tab_spec = pl.BlockSpec((ts, ATT_HEAD_DIM), lambda b, t: (t, 0))
    out_spec = pl.BlockSpec((1, ATT_PAIRS, dil, n, ATT_PAIR_W), lambda b, t: (b, 0, 0, t, 0))
    out_sds = jax.ShapeDtypeStruct((B, ATT_PAIRS, dil, S // dil, ATT_PAIR_W), BF16)
    head_sum = jnp.asarray(np.kron(np.eye(2, dtype=np.float32),
                                   np.ones((ATT_HEAD_DIM, ATT_HEAD_DIM), np.float32)), dtype=BF16)
    return pl.pallas_call(
        kern,
        name=f"att_proj_d{dil}",
        grid=(B, S // ts),
        in_specs=[
            pl.BlockSpec((1, ts, D), lambda b, t: (b, t, 0)),
            _const_spec((1, D)),
            pl.BlockSpec((D, 3 * ATT_WIDTH), lambda b, t: (0, group), pipeline_mode=pl.Buffered(1)),
            _const_spec((1, ATT_HEAD_DIM)),
            _const_spec((1, ATT_HEAD_DIM)),
            _const_spec((ATT_PAIR_W, ATT_PAIR_W)),
            tab_spec, tab_spec,
        ],
        out_specs=[out_spec, out_spec, out_spec],
        out_shape=[out_sds, out_sds, out_sds],
        scratch_shapes=[pltpu.VMEM((ts, D), BF16), pltpu.VMEM((D // LANES, ts, LANES), F32),
                        pltpu.VMEM((D // LANES, ATT_STRIDE_STEP, ts // ATT_STRIDE_STEP, LANES), F32)],
        compiler_params=pltpu.CompilerParams(
            dimension_semantics=("parallel", "parallel"), vmem_limit_bytes=VMEM_LIMIT_BYTES),
    )(x, norm_w.reshape(1, D), w_in, q_gain.reshape(1, -1), k_gain.reshape(1, -1), head_sum,
      *tables)


def _att_core_kernel(*refs, ts, seq):
    n_g = len(ATT_GROUPS)
    grp = [refs[7 * g:7 * g + 7] for g in range(n_g)]
    o_ref = refs[7 * n_g]
    og_s, lse_s, to_s, tl_s = refs[7 * n_g + 1:]
    step = ATT_STRIDE_STEP
    t = pl.program_id(1)
    side = ATT_SIDE
    hd = ATT_HEAD_DIM
    qb = ATT_QBLOCK
    nk = qb + 2 * side

    for g, (_, dil) in enumerate(ATT_GROUPS):
        q_ref, km_ref, kp_ref, kn_ref, vm_ref, vp_ref, vn_ref = grp[g]
        n = ts // dil
        n_sub = seq // dil

        def window(main_ref, prev_ref, next_ref, r, c0):
            parts = []
            if c0 == 0:
                parts.append(prev_ref[0, 0, r])
                lo = 0
            else:
                lo = c0 - side
            hi = min(c0 + qb + side, n)
            parts.append(main_ref[0, 0, r, lo:hi, :])
            if c0 + qb + side > n:
                parts.append(next_ref[0, 0, r])
            return parts[0] if len(parts) == 1 else jnp.concatenate(parts, axis=0)

        @pl.when(t >= 0)
        def _():
            a = lax.broadcasted_iota(jnp.int32, (qb, nk), 0)
            j = lax.broadcasted_iota(jnp.int32, (qb, nk), 1)
            band_bias = jnp.where((j - a >= 0) & (j - a <= 2 * side), 0.0, NEG_BIG).astype(F32)
            jrow = lax.broadcasted_iota(jnp.int32, (1, nk), 1)
            ones = jnp.ones((nk, hd), BF16)
            for r in range(dil):
                for c0 in range(0, n, qb):
                    base = t * n + c0 - side
                    key_ok = (jrow + base >= 0) & (jrow + base < n_sub)
                    bias = band_bias + jnp.where(key_ok, 0.0, NEG_BIG).astype(F32)
                    qv = q_ref[0, 0, r, c0:c0 + qb, :]
                    kv = window(km_ref, kp_ref, kn_ref, r, c0)
                    vv = window(vm_ref, vp_ref, vn_ref, r, c0)
                    for hh in range(2):
                        cols = slice(hh * hd, (hh + 1) * hd)
                        s = _dot_nt(qv[:, cols], kv[:, cols]) + bias
                        m = jnp.max(s, axis=-1, keepdims=True)
                        p = jnp.exp2(s - m).astype(BF16)
                        acc = _dot(p, jnp.concatenate([vv[:, cols], ones], axis=1))
                        den = acc[:, hd:]
                        o = acc[:, :hd] / den
                        lse = m * LN2 + jnp.log(den)
                        if dil == 1:
                            og_s[g, hh, c0:c0 + qb, :] = o
                            lse_s[g, hh, c0:c0 + qb, :] = lse
                        elif dil <= step:
                            rows = pl.ds(c0 * dil + r, qb, stride=dil)
                            og_s[g, hh, rows, :] = o
                            lse_s[g, hh, rows, :] = lse
                        else:
                            hi, lo = divmod(r, step)
                            rows = pl.ds(c0 * (dil // step) + hi, qb, stride=dil // step)
                            to_s[hh, lo, rows, :] = o
                            tl_s[hh, lo, rows, :] = lse
            if dil > step:
                for hh in range(2):
                    for lo in range(step):
                        rows = pl.ds(lo, ts // step, stride=step)
                        og_s[g, hh, rows, :] = to_s[hh, lo]
                        lse_s[g, hh, rows, :] = tl_s[hh, lo]

    for c0 in range(0, ts, ATT_MERGE_ROWS):
        rows = slice(c0, c0 + ATT_MERGE_ROWS)
        ys = []
        for hh in range(2):
            lses = [lse_s[g, hh, rows, :] for g in range(n_g)]
            m = functools.reduce(jnp.maximum, lses)
            ws = [jnp.exp(l - m) for l in lses]
            den = functools.reduce(lambda u, v: u + v, ws)
            num = functools.reduce(lambda u, v: u + v,
                                   [w * og_s[g, hh, rows, :] for g, w in enumerate(ws)])
            ys.append(num / den)
        o_ref[0, rows, :] = jnp.concatenate(ys, axis=1).astype(BF16)


def _att_core(qkv, S):
    B = qkv[0][0].shape[0]
    ts = ATT_TILE
    side = ATT_SIDE
    pw = ATT_PAIR_W
    in_specs, args = [], []
    for (_, dil), (q, k, v) in zip(ATT_GROUPS, qkv):
        n = ts // dil
        per = n // side
        last = S // dil // side - 1
        main = pl.BlockSpec((1, 1, dil, n, pw), lambda b, t, h: (b, h, 0, t, 0))
        prev = pl.BlockSpec((1, 1, dil, side, pw),
                            lambda b, t, h, per=per: (b, h, 0, jnp.maximum(t * per - 1, 0), 0))
        nxt = pl.BlockSpec((1, 1, dil, side, pw),
                           lambda b, t, h, per=per, last=last: (b, h, 0, jnp.minimum((t + 1) * per, last), 0))
        in_specs += [main, main, prev, nxt, main, prev, nxt]
        args += [q, k, k, k, v, v, v]
    kern = functools.partial(_att_core_kernel, ts=ts, seq=S)
    n_g = len(ATT_GROUPS)
    return pl.pallas_call(
        kern,
        name="att_core",
        grid=(B, S // ts, ATT_PAIRS),
        in_specs=in_specs,
        out_specs=pl.BlockSpec((1, ts, pw), lambda b, t, h: (b, t, h)),
        out_shape=jax.ShapeDtypeStruct((B, S, ATT_WIDTH), BF16),
        scratch_shapes=[
            pltpu.VMEM((n_g, 2, ts, ATT_HEAD_DIM), F32),
            pltpu.VMEM((n_g, 2, ts, ATT_HEAD_DIM), F32),
            pltpu.VMEM((2, ATT_STRIDE_STEP, ts // ATT_STRIDE_STEP, ATT_HEAD_DIM), F32),
            pltpu.VMEM((2, ATT_STRIDE_STEP, ts // ATT_STRIDE_STEP, ATT_HEAD_DIM), F32),
        ],
        compiler_params=pltpu.CompilerParams(
            dimension_semantics=("parallel", "parallel", "parallel"),
            vmem_limit_bytes=VMEM_LIMIT_BYTES),
    )(*args)


def _att_out_kernel(x_ref, a_ref, nw_ref, wgate_ref, wout_ref, o_ref):
    h = _rms(x_ref[0], nw_ref[...]).astype(BF16)
    gate = _dot(h, wgate_ref[...])
    y = (a_ref[0].astype(F32) * _silu(gate)).astype(BF16)
    o_ref[0] = x_ref[0] + _dot(y, wout_ref[...])


def _att_out(x, att, norm_w, w_in, gate_block, w_out):
    B, S, D = x.shape
    ts = ATT_OUT_TILE
    return pl.pallas_call(
        _att_out_kernel,
        name="att_out",
        grid=(B, S // ts),
        in_specs=[
            pl.BlockSpec((1, ts, D), lambda b, t: (b, t, 0)),
            pl.BlockSpec((1, ts, ATT_WIDTH), lambda b, t: (b, t, 0)),
            _const_spec((1, D)),
            pl.BlockSpec((D, ATT_WIDTH), lambda b, t: (0, gate_block), pipeline_mode=pl.Buffered(1)),
            _const_spec((ATT_WIDTH, D)),
        ],
        out_specs=pl.BlockSpec((1, ts, D), lambda b, t: (b, t, 0)),
        out_shape=jax.ShapeDtypeStruct((B, S, D), F32),
        compiler_params=pltpu.CompilerParams(
            dimension_semantics=("parallel", "parallel"), vmem_limit_bytes=VMEM_LIMIT_BYTES),
    )(x, att, norm_w.reshape(1, D), w_in, w_out)


def _att_layer(x, norm_w, w_in, q_norm, k_norm, w_out):
    B, S, D = x.shape
    n_g = len(ATT_GROUPS)
    tables = _att_rope_tables(S)
    perm = _att_head_perm()
    head = np.arange(w_in.shape[1] // ATT_HEAD_DIM)
    is_qk = ((head // ATT_HEADS) % 3 < 2) & (head < 3 * n_g * ATT_HEADS)
    w_heads = w_in.reshape(D, -1, ATT_HEAD_DIM)
    cuts = [0, ROPE_DIM // 2, ROPE_DIM, ROPE_DIM + (ATT_HEAD_DIM - ROPE_DIM) // 2, ATT_HEAD_DIM]
    parts = [w_heads[..., lo:hi] for lo, hi in zip(cuts[:-1], cuts[1:])]
    w_perm = jnp.concatenate([parts[0], parts[2], parts[1], parts[3]], axis=-1)
    w_in = jnp.where(is_qk[None, :, None], w_perm, w_heads).astype(BF16).reshape(D, -1)
    q_norm, k_norm = q_norm[:, perm], k_norm[:, perm]
    qkv = []
    for g, (_, dil) in enumerate(ATT_GROUPS):
        qkv.append(_att_proj(x, norm_w, w_in, g, q_norm[g], k_norm[g], tables, dil))
    att = _att_core(qkv, S)
    return _att_out(x, att, norm_w, w_in, 3 * n_g, w_out.astype(BF16))


def _ret_rope_tables(S):
    return _rope_angles(S, RET_QK_DIM // 2, RET_QK_DIM, RET_THETA)


def _ret_rotate(val, cos, sin, scale):
    half = RET_QK_DIM // 2
    x1, x2 = val[:, :half], val[:, half:]
    out = jnp.concatenate([x1 * cos - x2 * sin, x2 * cos + x1 * sin], axis=1)
    return out * scale if scale != 1.0 else out


def _ret_decays(lg, reverse):
    C = RET_CHUNK
    i = lax.broadcasted_iota(jnp.int32, (C, C), 0)
    jj = lax.broadcasted_iota(jnp.int32, (C, C), 1)
    col = lax.broadcasted_iota(jnp.int32, (C, 1), 0).astype(F32)
    if reverse:
        dist, keep = jj - i, jj > i
        q_dec = jnp.exp((C - col) * lg)
        k_dec = jnp.exp(col * lg)
    else:
        dist, keep = i - jj, i >= jj
        q_dec = jnp.exp((col + 1.0) * lg)
        k_dec = jnp.exp((C - 1.0 - col) * lg)
    decay = jnp.where(keep, jnp.exp(jnp.where(keep, dist, 0).astype(F32) * lg), 0.0)
    chunk_dec = jnp.exp(jnp.full((1, 1), float(C), F32) * lg)
    return decay, q_dec, k_dec, chunk_dec


def _ret_project_qk(h_s, w_ref, cos_ref, sin_ref, q_s, k_s):
    cos, sin = cos_ref[...], sin_ref[...]
    for hd in range(RET_HEADS):
        qc = slice(hd * RET_QK_DIM, (hd + 1) * RET_QK_DIM)
        kc = slice(RET_QK + hd * RET_QK_DIM, RET_QK + (hd + 1) * RET_QK_DIM)
        q_s[hd] = _ret_rotate(_dot(h_s[...], w_ref[:, qc]), cos, sin, 1.0)
        k_s[hd] = _ret_rotate(_dot(h_s[...], w_ref[:, kc]), cos, sin, RET_QK_DIM ** -0.5)


def _ret_scan_chunk(rows, lg_row, q_s, k_s, v_at, st_s, reverse, filler=None):
    heads = range(RET_HEADS)
    decs = [_ret_decays(lg_row(hd), reverse) for hd in heads]
    q = [q_s[hd, rows, :] for hd in heads]
    k = [k_s[hd, rows, :] for hd in heads]
    v = [v_at(hd) for hd in heads]
    inner = [_dot_nt(q[hd].astype(BF16), k[hd].astype(BF16)) for hd in heads]
    kv = [_dot_tn((k[hd] * decs[hd][2]).astype(BF16), v[hd]) for hd in heads]
    extra = filler() if filler is not None else None
    outs = []
    for hd in heads:
        decay, q_dec, _, chunk_dec = decs[hd]
        state = st_s[hd]
        outs.append(_dot((inner[hd] * decay).astype(BF16), v[hd])
                    + _dot((q[hd] * q_dec).astype(BF16), state.astype(BF16)))
        st_s[hd] = state * chunk_dec + kv[hd]
    return outs, extra


def _ret_fwd_kernel(lg_ref, x_ref, nw_ref, w_ref, wv_ref, cos_ref, sin_ref, f_ref, v_ref,
                    h_s, q_s, k_s, st_s, *, ts):
    @pl.when(pl.program_id(1) == 0)
    def _():
        st_s[...] = jnp.zeros_like(st_s)

    h_s[...] = _rms(x_ref[0], nw_ref[...]).astype(BF16)
    _ret_project_qk(h_s, w_ref, cos_ref, sin_ref, q_s, k_s)
    for hd in range(RET_HEADS):
        vc = slice(hd * RET_V_DIM, (hd + 1) * RET_V_DIM)
        v_ref[0, :, vc] = _dot(h_s[...], wv_ref[:, vc]).astype(BF16)
    for c in range(ts // RET_CHUNK):
        rows = slice(c * RET_CHUNK, (c + 1) * RET_CHUNK)
        outs, _ = _ret_scan_chunk(
            rows, lambda hd: lg_ref[0, hd], q_s, k_s,
            lambda hd: v_ref[0, rows, hd * RET_V_DIM:(hd + 1) * RET_V_DIM], st_s, False)
        for hd in range(RET_HEADS):
            f_ref[0, rows, hd * RET_V_DIM:(hd + 1) * RET_V_DIM] = outs[hd]


def _ret_bwd_kernel(lg_ref, x_ref, nw_ref, w_ref, wg_ref, cos_ref, sin_ref, f_ref, v_ref, wout_ref,
                    o_ref, h_s, q_s, k_s, y_s, st_s, *, ts):
    @pl.when(pl.program_id(1) == 0)
    def _():
        st_s[...] = jnp.zeros_like(st_s)

    h_s[...] = _rms(x_ref[0], nw_ref[...]).astype(BF16)
    _ret_project_qk(h_s, w_ref, cos_ref, sin_ref, q_s, k_s)
    for c in reversed(range(ts // RET_CHUNK)):
        rows = slice(c * RET_CHUNK, (c + 1) * RET_CHUNK)

        def gates():
            return [_dot(h_s[rows, :], wg_ref[:, hd * RET_V_DIM:(hd + 1) * RET_V_DIM])
                    for hd in range(RET_HEADS)]

        outs, gate = _ret_scan_chunk(
            rows, lambda hd: lg_ref[1, hd], q_s, k_s,
            lambda hd: v_ref[0, rows, hd * RET_V_DIM:(hd + 1) * RET_V_DIM], st_s, True, gates)
        for hd in range(RET_HEADS):
            oc = slice(hd * RET_V_DIM, (hd + 1) * RET_V_DIM)
            y = outs[hd] + f_ref[0, rows, oc]
            y = y * lax.rsqrt(jnp.mean(y * y, axis=-1, keepdims=True) + EPS)
            y_s[rows, oc] = (y * _silu(gate[hd])).astype(BF16)
    o_ref[0] = x_ref[0] + _dot(y_s[...], wout_ref[...])


def _ret_layer(x, norm_w, w_in, decay_exp, w_out):
    B, S, D = x.shape
    ts = RET_TILE
    nt = S // ts
    w_in = w_in.astype(BF16)

    def w_cols(j):
        return pl.BlockSpec((D, 2 * RET_QK), lambda b, t: (0, j), pipeline_mode=pl.Buffered(1))

    log_gamma = jnp.log1p(-jnp.exp2(-decay_exp.astype(F32)))
    cos, sin = _ret_rope_tables(S)
    half = RET_QK_DIM // 2
    smem = pl.BlockSpec(memory_space=pltpu.SMEM)
    cparams = pltpu.CompilerParams(
        dimension_semantics=("parallel", "arbitrary"), vmem_limit_bytes=VMEM_LIMIT_BYTES)
    state = pltpu.VMEM((RET_HEADS, RET_QK_DIM, RET_V_DIM), F32)
    qk_scratch = pltpu.VMEM((RET_HEADS, ts, RET_QK_DIM), F32)

    fwd, v = pl.pallas_call(
        functools.partial(_ret_fwd_kernel, ts=ts),
        name="ret_fwd",
        grid=(B, nt),
        in_specs=[
            smem,
            pl.BlockSpec((1, ts, D), lambda b, t: (b, t, 0)),
            _const_spec((1, D)),
            w_cols(0), w_cols(2),
            pl.BlockSpec((ts, half), lambda b, t: (t, 0)),
            pl.BlockSpec((ts, half), lambda b, t: (t, 0)),
        ],
        out_specs=[pl.BlockSpec((1, ts, RET_V), lambda b, t: (b, t, 0)),
                   pl.BlockSpec((1, ts, RET_V), lambda b, t: (b, t, 0))],
        out_shape=[jax.ShapeDtypeStruct((B, S, RET_V), F32),
                   jax.ShapeDtypeStruct((B, S, RET_V), BF16)],
        scratch_shapes=[pltpu.VMEM((ts, D), BF16), qk_scratch, qk_scratch, state],
        compiler_params=cparams,
    )(log_gamma, x, norm_w.reshape(1, D), w_in, w_in, cos, sin)

    rev = lambda b, t: (b, nt - 1 - t, 0)
    return pl.pallas_call(
        functools.partial(_ret_bwd_kernel, ts=ts),
        name="ret_bwd",
        grid=(B, nt),
        in_specs=[
            smem,
            pl.BlockSpec((1, ts, D), rev),
            _const_spec((1, D)),
            w_cols(1), w_cols(3),
            pl.BlockSpec((ts, half), lambda b, t: (nt - 1 - t, 0)),
            pl.BlockSpec((ts, half), lambda b, t: (nt - 1 - t, 0)),
            pl.BlockSpec((1, ts, RET_V), rev),
            pl.BlockSpec((1, ts, RET_V), rev),
            _const_spec((RET_V, D)),
        ],
        out_specs=pl.BlockSpec((1, ts, D), rev),
        out_shape=jax.ShapeDtypeStruct((B, S, D), F32),
        scratch_shapes=[pltpu.VMEM((ts, D), BF16), qk_scratch, qk_scratch,
                        pltpu.VMEM((ts, RET_V), BF16), state],
        compiler_params=cparams,
    )(log_gamma, x, norm_w.reshape(1, D), w_in, w_in, cos, sin, fwd, v, w_out.astype(BF16))


def kernel(x, pool_norm, pool_w_in, pool_w_group, pool_scale, pool_w_out,
           att_norm, att_w_in, att_q_norm, att_k_norm, att_w_out,
           ret_norm, ret_w_in, ret_decay, ret_w_out):
    depth = pool_norm.shape[0] + att_norm.shape[0] + ret_norm.shape[0]
    for layer in range(depth):
        kind, idx = layer % 3, layer // 3
        if kind == 0:
            x = _pool_layer(x, pool_norm[idx], pool_w_in[idx], pool_w_group[idx], pool_scale[idx],
                            pool_w_out[idx])
        elif kind == 1:
            x = _att_layer(x, att_norm[idx], att_w_in[idx], att_q_norm[idx], att_k_norm[idx],
                           att_w_out[idx])
        else:
            x = _ret_layer(x, ret_norm[idx], ret_w_in[idx], ret_decay[idx], ret_w_out[idx])
    return x
```

```python
import functools
import math

import jax
import jax.numpy as jnp
import numpy as np
from jax import lax
from jax.experimental import pallas as pl
from jax.experimental.pallas import tpu as pltpu

F32 = jnp.float32
BF16 = jnp.bfloat16

D_MODEL = 1024
EPS = 1e-6
NEG_BIG = -1e30
LOG2E = math.log2(math.e)
LN2 = math.log(2.0)

POOL_WINDOWS = (2, 4, 8, 16)
POOL_WIDTH = 2 * D_MODEL
POOL_GROUP = POOL_WIDTH // len(POOL_WINDOWS)

ATT_GROUPS = ((128, 1), (512, 4), (2048, 16))
ATT_HEAD_DIM = 128
ATT_HEADS = D_MODEL // ATT_HEAD_DIM
ATT_WIDTH = ATT_HEADS * ATT_HEAD_DIM
ATT_SIDE = 64
ROPE_DIM = ATT_HEAD_DIM // 4
ROPE_THETA = 500000.0

RET_HEADS = 4
RET_QK_DIM = D_MODEL // RET_HEADS
RET_V_DIM = 2 * D_MODEL // RET_HEADS
RET_QK = RET_HEADS * RET_QK_DIM
RET_V = RET_HEADS * RET_V_DIM
RET_CHUNK = 256
RET_THETA = 10000.0

V7X_VMEM_BYTES = 64 * 1024 * 1024
VMEM_LIMIT_BYTES = V7X_VMEM_BYTES - 8 * 1024 * 1024
LANES = 128
BF16_ROWS = 16

POOL_TILE = 512
POOL_HALO = BF16_ROWS
POOL_BLOCK = 128
ATT_PROJ_TILE = 1024
ATT_TILE = 2048
ATT_QBLOCK = 128
ATT_OUT_TILE = 1024
ATT_STRIDE_STEP = 4
ATT_MERGE_ROWS = 32
ATT_PAIRS = ATT_HEADS // 2
ATT_PAIR_W = 2 * ATT_HEAD_DIM
RET_TILE = 512


def _const_spec(shape):
    nd = len(shape)
    return pl.BlockSpec(shape, lambda *_: (0,) * nd, pipeline_mode=pl.Buffered(1))


def _rms(v, w):
    return v * lax.rsqrt(jnp.mean(v * v, axis=-1, keepdims=True) + EPS) * w


def _silu(g):
    return g * (1.0 / (1.0 + jnp.exp(-g)))


def _rope_angles(S, half, rot_dim, theta):
    inv_freq = (np.float32(1.0) / np.power(np.float32(theta),
                                           np.arange(half, dtype=np.float32) * np.float32(2.0 / rot_dim)))
    ang = np.arange(S, dtype=np.float32)[:, None] * inv_freq[None, :]
    return np.cos(ang).astype(np.float32), np.sin(ang).astype(np.float32)


def _dot(a, b):
    return jnp.dot(a, b, preferred_element_type=F32)


def _dot_nt(a, b):
    return lax.dot_general(a, b, (((1,), (1,)), ((), ())), preferred_element_type=F32)


def _dot_tn(a, b):
    return lax.dot_general(a, b, (((0,), (0,)), ((), ())), preferred_element_type=F32)


def _pool_kernel(x_ref, xp_ref, xn_ref, nw_ref, win_ref, wg_ref, sc_ref, wout_ref, band_ref,
                 o_ref, h_s, u_s, ub_s, d_s, y_s, *, ts, seq):
    t = pl.program_id(1)
    nt = pl.num_programs(1)
    nw = nw_ref[...]
    halo = POOL_HALO
    n_grp = len(POOL_WINDOWS)

    h_s[halo:halo + ts, :] = _rms(x_ref[0], nw).astype(BF16)
    hp = _rms(xp_ref[0], nw)
    h_s[0:halo, :] = jnp.where(t > 0, hp, 0.0).astype(BF16)
    hn = _rms(xn_ref[0], nw)
    h_s[halo + ts:, :] = jnp.where(t < nt - 1, hn, 0.0).astype(BF16)

    row = lax.broadcasted_iota(jnp.int32, (POOL_BLOCK, 1), 0)

    def project(g):
        ug = _dot(h_s[...], win_ref[:, g * POOL_GROUP:(g + 1) * POOL_GROUP])
        u_s[g] = ug
        ub_s[g] = ug.astype(BF16)

    def pool(g):
        window = POOL_WINDOWS[g]
        for b in range(ts // POOL_BLOCK):
            r0 = b * POOL_BLOCK
            wsum = _dot(band_ref[g], ub_s[g, r0:r0 + POOL_BLOCK + 2 * halo, :])
            pos = t * ts + r0 + row
            lo = jnp.maximum(pos - window // 2, 0)
            hi = jnp.minimum(pos - window // 2 + window, seq)
            diff = wsum / (hi - lo).astype(F32) - u_s[g, halo + r0:halo + r0 + POOL_BLOCK, :]
            d_s[g, r0:r0 + POOL_BLOCK, :] = diff.astype(BF16)

    def mix(g):
        cols = slice(g * POOL_GROUP, (g + 1) * POOL_GROUP)
        gate = _dot(h_s[halo:halo + ts, :], win_ref[:, POOL_WIDTH + g * POOL_GROUP:
                                                    POOL_WIDTH + (g + 1) * POOL_GROUP])
        mixed = _dot(d_s[g], wg_ref[g]) * sc_ref[:, cols]
        y_s[:, cols] = (mixed * _silu(gate)).astype(BF16)

    for step in range(n_grp + 2):
        if step < n_grp:
            project(step)
        if 0 <= step - 1 < n_grp:
            pool(step - 1)
        if 0 <= step - 2 < n_grp:
            mix(step - 2)
    o_ref[0] = x_ref[0] + _dot(y_s[...], wout_ref[...])


def _pool_band():
    i = np.arange(POOL_BLOCK)[:, None]
    j = np.arange(POOL_BLOCK + 2 * POOL_HALO)[None, :] - POOL_HALO
    bands = [((j - i >= -(w // 2)) & (j - i < w - w // 2)) for w in POOL_WINDOWS]
    return jnp.asarray(np.stack(bands).astype(np.float32), dtype=BF16)


def _pool_layer(x, norm_w, w_in, w_group, scale, w_out):
    B, S, D = x.shape
    ts, halo = POOL_TILE, POOL_HALO
    nt = S // ts
    hb = ts // halo
    n_hb = S // halo
    kern = functools.partial(_pool_kernel, ts=ts, seq=S)
    return pl.pallas_call(
        kern,
        name="pool_layer",
        grid=(B, nt),
        in_specs=[
            pl.BlockSpec((1, ts, D), lambda b, t: (b, t, 0)),
            pl.BlockSpec((1, halo, D), lambda b, t: (b, jnp.maximum(t * hb - 1, 0), 0)),
            pl.BlockSpec((1, halo, D), lambda b, t: (b, jnp.minimum((t + 1) * hb, n_hb - 1), 0)),
            _const_spec((1, D)),
            _const_spec((D, 2 * POOL_WIDTH)),
            _const_spec((len(POOL_WINDOWS), POOL_GROUP, POOL_GROUP)),
            _const_spec((1, POOL_WIDTH)),
            _const_spec((POOL_WIDTH, D)),
            _const_spec((len(POOL_WINDOWS), POOL_BLOCK, POOL_BLOCK + 2 * halo)),
        ],
        out_specs=pl.BlockSpec((1, ts, D), lambda b, t: (b, t, 0)),
        out_shape=jax.ShapeDtypeStruct((B, S, D), F32),
        scratch_shapes=[
            pltpu.VMEM((ts + 2 * halo, D), BF16),
            pltpu.VMEM((len(POOL_WINDOWS), ts + 2 * halo, POOL_GROUP), F32),
            pltpu.VMEM((len(POOL_WINDOWS), ts + 2 * halo, POOL_GROUP), BF16),
            pltpu.VMEM((len(POOL_WINDOWS), ts, POOL_GROUP), BF16),
            pltpu.VMEM((ts, POOL_WIDTH), BF16),
        ],
        compiler_params=pltpu.CompilerParams(
            dimension_semantics=("parallel", "parallel"), vmem_limit_bytes=VMEM_LIMIT_BYTES),
    )(x, x, x, norm_w.reshape(1, D), w_in.astype(BF16), w_group.astype(BF16),
      scale.reshape(1, POOL_WIDTH), w_out.astype(BF16), _pool_band())


def _att_head_perm():
    half = ROPE_DIM // 2
    rest = (ATT_HEAD_DIM - ROPE_DIM) // 2
    order = (list(range(half)) + list(range(ROPE_DIM, ROPE_DIM + rest))
             + list(range(half, ROPE_DIM)) + list(range(ROPE_DIM + rest, ATT_HEAD_DIM)))
    return np.array(order, dtype=np.int32)


def _att_rope_tables(S):
    half = ROPE_DIM // 2
    cos, sin = _rope_angles(S, half, ROPE_DIM, ROPE_THETA)
    pad = ATT_HEAD_DIM // 2 - half
    one, zero = np.ones((S, pad), np.float32), np.zeros((S, pad), np.float32)
    return (np.concatenate([cos, one, cos, one], axis=1),
            np.concatenate([-sin, zero, sin, zero], axis=1))


def _att_proj_kernel(x_ref, nw_ref, w_ref, qn_ref, kn_ref, hsum_ref, cos_ref, sin_ref,
                     q_ref, k_ref, v_ref, h_s, hf_s, hg_s, *, ts, dil):
    hd = ATT_HEAD_DIM
    n = ts // dil
    hn = _rms(x_ref[0], nw_ref[...])
    if dil == 1:
        h_s[...] = hn.astype(BF16)
    else:
        n_slab = hn.shape[1] // LANES
        for c in range(n_slab):
            hf_s[c] = hn[:, c * LANES:(c + 1) * LANES]
        if dil > ATT_STRIDE_STEP:
            step = ATT_STRIDE_STEP
            for c in range(n_slab):
                for lo in range(step):
                    hg_s[c, lo] = hf_s[c, pl.ds(lo, ts // step, stride=step), :]
            for r in range(dil):
                hi, lo = divmod(r, step)
                rows = pl.ds(hi, n, stride=dil // step)
                h_s[r * n:(r + 1) * n, :] = jnp.concatenate(
                    [hg_s[c, lo, rows, :] for c in range(n_slab)], axis=1).astype(BF16)
        else:
            for r in range(dil):
                rows = pl.ds(r, n, stride=dil)
                h_s[r * n:(r + 1) * n, :] = jnp.concatenate(
                    [hf_s[c, rows, :] for c in range(n_slab)], axis=1).astype(BF16)
    cos_t, sin_t = cos_ref[...], sin_ref[...]

    def emit(val, out_ref, pair, hh):
        out_ref[0, pair, :, :, hh * hd:(hh + 1) * hd] = val.astype(BF16).reshape(dil, n, hd)

    def qk_head(val, inv_rms, gain):
        y = val * inv_rms * gain
        return y * cos_t + pltpu.roll(y, hd // 2, 1) * sin_t

    gains = (qn_ref[...] * (hd ** -0.5 * LOG2E), kn_ref[...])
    out_refs = (q_ref, k_ref, v_ref)
    n_pairs = 3 * ATT_PAIRS

    def project(i):
        return _dot(h_s[...], w_ref[:, i * ATT_PAIR_W:(i + 1) * ATT_PAIR_W])

    nxt = project(0)
    for i in range(n_pairs):
        proj = nxt
        if i + 1 < n_pairs:
            nxt = project(i + 1)
        kind, pair = divmod(i, ATT_PAIRS)
        if kind < 2:
            ssq = _dot((proj * proj).astype(BF16), hsum_ref[...])
            inv_rms = lax.rsqrt(ssq * (1.0 / hd) + EPS)
        for hh in range(2):
            lanes = slice(hh * hd, (hh + 1) * hd)
            if kind < 2:
                emit(qk_head(proj[:, lanes], inv_rms[:, lanes], gains[kind]), out_refs[kind], pair, hh)
            else:
                emit(proj[:, lanes], v_ref, pair, hh)


def _att_proj(x, norm_w, w_in, group, q_gain, k_gain, tables, dil):
    B, S, D = x.shape
    ts = ATT_PROJ_TILE
    n = ts // dil
    kern = functools.partial(_att_proj_kernel, ts=ts, dil=dil)
    tables = [tab.reshape(S // ts, n, dil, ATT_HEAD_DIM).transpose(0, 2, 1, 3).reshape(S, ATT_HEAD_DIM)
              for tab in tables]
    tab_spec = pl.BlockSpec((ts, ATT_HEAD_DIM), lambda b, t: (t, 0))
    out_spec = pl.BlockSpec((1, ATT_PAIRS, dil, n, ATT_PAIR_W), lambda b, t: (b, 0, 0, t, 0))
    out_sds = jax.ShapeDtypeStruct((B, ATT_PAIRS, dil, S // dil, ATT_PAIR_W), BF16)
    head_sum = jnp.asarray(np.kron(np.eye(2, dtype=np.float32),
                                   np.ones((ATT_HEAD_DIM, ATT_HEAD_DIM), np.float32)), dtype=BF16)
    return pl.pallas_call(
        kern,
        name=f"att_proj_d{dil}",
        grid=(B, S // ts),
        in_specs=[
            pl.BlockSpec((1, ts, D), lambda b, t: (b, t, 0)),
            _const_spec((1, D)),
            pl.BlockSpec((D, 3 * ATT_WIDTH), lambda b, t: (0, group), pipeline_mode=pl.Buffered(1)),
            _const_spec((1, ATT_HEAD_DIM)),
            _const_spec((1, ATT_HEAD_DIM)),
            _const_spec((ATT_PAIR_W, ATT_PAIR_W)),
            tab_spec, tab_spec,
        ],
        out_specs=[out_spec, out_spec, out_spec],
        out_shape=[out_sds, out_sds, out_sds],
        scratch_shapes=[pltpu.VMEM((ts, D), BF16), pltpu.VMEM((D // LANES, ts, LANES), F32),
                        pltpu.VMEM((D // LANES, ATT_STRIDE_STEP, ts // ATT_STRIDE_STEP, LANES), F32)],
        compiler_params=pltpu.CompilerParams(
            dimension_semantics=("parallel", "parallel"), vmem_limit_bytes=VMEM_LIMIT_BYTES),
    )(x, norm_w.reshape(1, D), w_in, q_gain.reshape(1, -1), k_gain.reshape(1, -1), head_sum,
      *tables)


def _att_core_kernel(*refs, ts, seq):
    n_g = len(ATT_GROUPS)
    grp = [refs[7 * g:7 * g + 7] for g in range(n_g)]
    o_ref = refs[7 * n_g]
    og_s, den_s, max_s, to_s, td_s, tm_s = refs[7 * n_g + 1:]
    step = ATT_STRIDE_STEP
    t = pl.program_id(1)
    side = ATT_SIDE
    hd = ATT_HEAD_DIM
    qb = ATT_QBLOCK
    nk = qb + 2 * side

    for g, (_, dil) in enumerate(ATT_GROUPS):
        q_ref, km_ref, kp_ref, kn_ref, vm_ref, vp_ref, vn_ref = grp[g]
        n = ts // dil
        n_sub = seq // dil

        def window(main_ref, prev_ref, next_ref, r, c0):
            parts = []
            if c0 == 0:
                parts.append(prev_ref[0, 0, r])
                lo = 0
            else:
                lo = c0 - side
            hi = min(c0 + qb + side, n)
            parts.append(main_ref[0, 0, r, lo:hi, :])
            if c0 + qb + side > n:
                parts.append(next_ref[0, 0, r])
            return parts[0] if len(parts) == 1 else jnp.concatenate(parts, axis=0)

        @pl.when(t >= 0)
        def _():
            a = lax.broadcasted_iota(jnp.int32, (qb, nk), 0)
            j = lax.broadcasted_iota(jnp.int32, (qb, nk), 1)
            band_bias = jnp.where((j - a >= 0) & (j - a <= 2 * side), 0.0, NEG_BIG).astype(F32)
            jrow = lax.broadcasted_iota(jnp.int32, (1, nk), 1)
            ones = jnp.ones((nk, hd), BF16)
            for r in range(dil):
                for c0 in range(0, n, qb):
                    base = t * n + c0 - side
                    key_ok = (jrow + base >= 0) & (jrow + base < n_sub)
                    bias = band_bias + jnp.where(key_ok, 0.0, NEG_BIG).astype(F32)
                    qv = q_ref[0, 0, r, c0:c0 + qb, :]
                    kv = window(km_ref, kp_ref, kn_ref, r, c0)
                    vv = window(vm_ref, vp_ref, vn_ref, r, c0)
                    for hh in range(2):
                        cols = slice(hh * hd, (hh + 1) * hd)
                        s = _dot_nt(qv[:, cols], kv[:, cols]) + bias
                        m = jnp.max(s, axis=-1, keepdims=True)
                        p = jnp.exp2(s - m).astype(BF16)
                        acc = _dot(p, jnp.concatenate([vv[:, cols], ones], axis=1))
                        vals = (acc[:, :hd], acc[:, hd:], jnp.broadcast_to(m, (qb, hd)))
                        if dil == 1:
                            dsts, rows = (og_s.at[g, hh], den_s.at[g, hh], max_s.at[g, hh]), slice(c0, c0 + qb)
                        elif dil <= step:
                            dsts = (og_s.at[g, hh], den_s.at[g, hh], max_s.at[g, hh])
                            rows = pl.ds(c0 * dil + r, qb, stride=dil)
                        else:
                            hi, lo = divmod(r, step)
                            dsts = (to_s.at[hh, lo], td_s.at[hh, lo], tm_s.at[hh, lo])
                            rows = pl.ds(c0 * (dil // step) + hi, qb, stride=dil // step)
                        for dst, val in zip(dsts, vals):
                            dst[rows, :] = val
            if dil > step:
                for hh in range(2):
                    for lo in range(step):
                        rows = pl.ds(lo, ts // step, stride=step)
                        og_s[g, hh, rows, :] = to_s[hh, lo]
                        den_s[g, hh, rows, :] = td_s[hh, lo]
                        max_s[g, hh, rows, :] = tm_s[hh, lo]

    for c0 in range(0, ts, ATT_MERGE_ROWS):
        rows = slice(c0, c0 + ATT_MERGE_ROWS)
        ys = []
        for hh in range(2):
            maxes = [max_s[g, hh, rows, :] for g in range(n_g)]
            m = functools.reduce(jnp.maximum, maxes)
            ws = [jnp.exp2(mg - m) for mg in maxes]
            den = functools.reduce(lambda u, v: u + v,
                                   [w * den_s[g, hh, rows, :] for g, w in enumerate(ws)])
            num = functools.reduce(lambda u, v: u + v,
                                   [w * og_s[g, hh, rows, :] for g, w in enumerate(ws)])
            ys.append(num / den)
        o_ref[0, rows, :] = jnp.concatenate(ys, axis=1).astype(BF16)


def _att_core(qkv, S):
    B = qkv[0][0].shape[0]
    ts = ATT_TILE
    side = ATT_SIDE
    pw = ATT_PAIR_W
    in_specs, args = [], []
    for (_, dil), (q, k, v) in zip(ATT_GROUPS, qkv):
        n = ts // dil
        per = n // side
        last = S // dil // side - 1
        main = pl.BlockSpec((1, 1, dil, n, pw), lambda b, t, h: (b, h, 0, t, 0))
        prev = pl.BlockSpec((1, 1, dil, side, pw),
                            lambda b, t, h, per=per: (b, h, 0, jnp.maximum(t * per - 1, 0), 0))
        nxt = pl.BlockSpec((1, 1, dil, side, pw),
                           lambda b, t, h, per=per, last=last: (b, h, 0, jnp.minimum((t + 1) * per, last), 0))
        in_specs += [main, main, prev, nxt, main, prev, nxt]
        args += [q, k, k, k, v, v, v]
    kern = functools.partial(_att_core_kernel, ts=ts, seq=S)
    n_g = len(ATT_GROUPS)
    return pl.pallas_call(
        kern,
        name="att_core",
        grid=(B, S // ts, ATT_PAIRS),
        in_specs=in_specs,
        out_specs=pl.BlockSpec((1, ts, pw), lambda b, t, h: (b, t, h)),
        out_shape=jax.ShapeDtypeStruct((B, S, ATT_WIDTH), BF16),
        scratch_shapes=[
            pltpu.VMEM((n_g, 2, ts, ATT_HEAD_DIM), F32),
            pltpu.VMEM((n_g, 2, ts, ATT_HEAD_DIM), F32),
            pltpu.VMEM((n_g, 2, ts, ATT_HEAD_DIM), F32),
            pltpu.VMEM((2, ATT_STRIDE_STEP, ts // ATT_STRIDE_STEP, ATT_HEAD_DIM), F32),
            pltpu.VMEM((2, ATT_STRIDE_STEP, ts // ATT_STRIDE_STEP, ATT_HEAD_DIM), F32),
            pltpu.VMEM((2, ATT_STRIDE_STEP, ts // ATT_STRIDE_STEP, ATT_HEAD_DIM), F32),
        ],
        compiler_params=pltpu.CompilerParams(
            dimension_semantics=("parallel", "parallel", "parallel"),
            vmem_limit_bytes=VMEM_LIMIT_BYTES),
    )(*args)


def _att_out_kernel(x_ref, a_ref, nw_ref, wgate_ref, wout_ref, o_ref):
    h = _rms(x_ref[0], nw_ref[...]).astype(BF16)
    gate = _dot(h, wgate_ref[...])
    y = (a_ref[0].astype(F32) * _silu(gate)).astype(BF16)
    o_ref[0] = x_ref[0] + _dot(y, wout_ref[...])


def _att_out(x, att, norm_w, w_in, gate_block, w_out):
    B, S, D = x.shape
    ts = ATT_OUT_TILE
    return pl.pallas_call(
        _att_out_kernel,
        name="att_out",
        grid=(B, S // ts),
        in_specs=[
            pl.BlockSpec((1, ts, D), lambda b, t: (b, t, 0)),
            pl.BlockSpec((1, ts, ATT_WIDTH), lambda b, t: (b, t, 0)),
            _const_spec((1, D)),
            pl.BlockSpec((D, ATT_WIDTH), lambda b, t: (0, gate_block), pipeline_mode=pl.Buffered(1)),
            _const_spec((ATT_WIDTH, D)),
        ],
        out_specs=pl.BlockSpec((1, ts, D), lambda b, t: (b, t, 0)),
        out_shape=jax.ShapeDtypeStruct((B, S, D), F32),
        compiler_params=pltpu.CompilerParams(
            dimension_semantics=("parallel", "parallel"), vmem_limit_bytes=VMEM_LIMIT_BYTES),
    )(x, att, norm_w.reshape(1, D), w_in, w_out)


def _att_layer(x, norm_w, w_in, q_norm, k_norm, w_out):
    B, S, D = x.shape
    n_g = len(ATT_GROUPS)
    tables = _att_rope_tables(S)
    perm = _att_head_perm()
    head = np.arange(w_in.shape[1] // ATT_HEAD_DIM)
    is_qk = ((head // ATT_HEADS) % 3 < 2) & (head < 3 * n_g * ATT_HEADS)
    w_heads = w_in.reshape(D, -1, ATT_HEAD_DIM)
    cuts = [0, ROPE_DIM // 2, ROPE_DIM, ROPE_DIM + (ATT_HEAD_DIM - ROPE_DIM) // 2, ATT_HEAD_DIM]
    parts = [w_heads[..., lo:hi] for lo, hi in zip(cuts[:-1], cuts[1:])]
    w_perm = jnp.concatenate([parts[0], parts[2], parts[1], parts[3]], axis=-1)
    w_in = jnp.where(is_qk[None, :, None], w_perm, w_heads).astype(BF16).reshape(D, -1)
    q_norm, k_norm = q_norm[:, perm], k_norm[:, perm]
    qkv = []
    for g, (_, dil) in enumerate(ATT_GROUPS):
        qkv.append(_att_proj(x, norm_w, w_in, g, q_norm[g], k_norm[g], tables, dil))
    att = _att_core(qkv, S)
    return _att_out(x, att, norm_w, w_in, 3 * n_g, w_out.astype(BF16))


def _ret_rope_tables(S):
    return _rope_angles(S, RET_QK_DIM // 2, RET_QK_DIM, RET_THETA)


def _ret_rotate(val, cos, sin, scale):
    half = RET_QK_DIM // 2
    x1, x2 = val[:, :half], val[:, half:]
    out = jnp.concatenate([x1 * cos - x2 * sin, x2 * cos + x1 * sin], axis=1)
    return out * scale if scale != 1.0 else out


def _ret_decays(lg, reverse):
    C = RET_CHUNK
    i = lax.broadcasted_iota(jnp.int32, (C, C), 0)
    jj = lax.broadcasted_iota(jnp.int32, (C, C), 1)
    col = lax.broadcasted_iota(jnp.int32, (C, 1), 0).astype(F32)
    if reverse:
        dist, keep = jj - i, jj > i
        q_dec = jnp.exp((C - col) * lg)
        k_dec = jnp.exp(col * lg)
    else:
        dist, keep = i - jj, i >= jj
        q_dec = jnp.exp((col + 1.0) * lg)
        k_dec = jnp.exp((C - 1.0 - col) * lg)
    decay = jnp.where(keep, jnp.exp(jnp.where(keep, dist, 0).astype(F32) * lg), 0.0)
    chunk_dec = jnp.exp(jnp.full((1, 1), float(C), F32) * lg)
    return decay, q_dec, k_dec, chunk_dec


def _ret_project_qk(h_s, w_ref, cos_ref, sin_ref, q_s, k_s):
    cos, sin = cos_ref[...], sin_ref[...]
    for hd in range(RET_HEADS):
        qc = slice(hd * RET_QK_DIM, (hd + 1) * RET_QK_DIM)
        kc = slice(RET_QK + hd * RET_QK_DIM, RET_QK + (hd + 1) * RET_QK_DIM)
        q_s[hd] = _ret_rotate(_dot(h_s[...], w_ref[:, qc]), cos, sin, 1.0)
        k_s[hd] = _ret_rotate(_dot(h_s[...], w_ref[:, kc]), cos, sin, RET_QK_DIM ** -0.5)


def _ret_scan_chunk(rows, lg_row, q_s, k_s, v_at, st_s, reverse, filler=None):
    heads = range(RET_HEADS)
    decs = [_ret_decays(lg_row(hd), reverse) for hd in heads]
    q = [q_s[hd, rows, :] for hd in heads]
    k = [k_s[hd, rows, :] for hd in heads]
    v = [v_at(hd) for hd in heads]
    inner = [_dot_nt(q[hd].astype(BF16), k[hd].astype(BF16)) for hd in heads]
    kv = [_dot_tn((k[hd] * decs[hd][2]).astype(BF16), v[hd]) for hd in heads]
    extra = filler() if filler is not None else None
    outs = []
    for hd in heads:
        decay, q_dec, _, chunk_dec = decs[hd]
        state = st_s[hd]
        outs.append(_dot((inner[hd] * decay).astype(BF16), v[hd])
                    + _dot((q[hd] * q_dec).astype(BF16), state.astype(BF16)))
        st_s[hd] = state * chunk_dec + kv[hd]
    return outs, extra


def _ret_fwd_kernel(lg_ref, x_ref, nw_ref, w_ref, wv_ref, cos_ref, sin_ref, f_ref, v_ref,
                    h_s, q_s, k_s, st_s, *, ts):
    @pl.when(pl.program_id(1) == 0)
    def _():
        st_s[...] = jnp.zeros_like(st_s)

    h_s[...] = _rms(x_ref[0], nw_ref[...]).astype(BF16)
    _ret_project_qk(h_s, w_ref, cos_ref, sin_ref, q_s, k_s)
    for hd in range(RET_HEADS):
        vc = slice(hd * RET_V_DIM, (hd + 1) * RET_V_DIM)
        v_ref[0, :, vc] = _dot(h_s[...], wv_ref[:, vc]).astype(BF16)
    for c in range(ts // RET_CHUNK):
        rows = slice(c * RET_CHUNK, (c + 1) * RET_CHUNK)
        outs, _ = _ret_scan_chunk(
            rows, lambda hd: lg_ref[0, hd], q_s, k_s,
            lambda hd: v_ref[0, rows, hd * RET_V_DIM:(hd + 1) * RET_V_DIM], st_s, False)
        for hd in range(RET_HEADS):
            f_ref[0, rows, hd * RET_V_DIM:(hd + 1) * RET_V_DIM] = outs[hd]


def _ret_bwd_kernel(lg_ref, x_ref, nw_ref, w_ref, wg_ref, cos_ref, sin_ref, f_ref, v_ref, wout_ref,
                    o_ref, h_s, q_s, k_s, y_s, st_s, *, ts):
    @pl.when(pl.program_id(1) == 0)
    def _():
        st_s[...] = jnp.zeros_like(st_s)

    h_s[...] = _rms(x_ref[0], nw_ref[...]).astype(BF16)
    _ret_project_qk(h_s, w_ref, cos_ref, sin_ref, q_s, k_s)
    for c in reversed(range(ts // RET_CHUNK)):
        rows = slice(c * RET_CHUNK, (c + 1) * RET_CHUNK)

        def gates():
            return [_dot(h_s[rows, :], wg_ref[:, hd * RET_V_DIM:(hd + 1) * RET_V_DIM])
                    for hd in range(RET_HEADS)]

        outs, gate = _ret_scan_chunk(
            rows, lambda hd: lg_ref[1, hd], q_s, k_s,
            lambda hd: v_ref[0, rows, hd * RET_V_DIM:(hd + 1) * RET_V_DIM], st_s, True, gates)
        for hd in range(RET_HEADS):
            oc = slice(hd * RET_V_DIM, (hd + 1) * RET_V_DIM)
            y = outs[hd] + f_ref[0, rows, oc]
            y = y * lax.rsqrt(jnp.mean(y * y, axis=-1, keepdims=True) + EPS)
            y_s[rows, oc] = (y * _silu(gate[hd])).astype(BF16)
    o_ref[0] = x_ref[0] + _dot(y_s[...], wout_ref[...])


def _ret_layer(x, norm_w, w_in, decay_exp, w_out):
    B, S, D = x.shape
    ts = RET_TILE
    nt = S // ts
    w_in = w_in.astype(BF16)

    def w_cols(j):
        return pl.BlockSpec((D, 2 * RET_QK), lambda b, t: (0, j), pipeline_mode=pl.Buffered(1))

    log_gamma = jnp.log1p(-jnp.exp2(-decay_exp.astype(F32)))
    cos, sin = _ret_rope_tables(S)
    half = RET_QK_DIM // 2
    smem = pl.BlockSpec(memory_space=pltpu.SMEM)
    cparams = pltpu.CompilerParams(
        dimension_semantics=("parallel", "arbitrary"), vmem_limit_bytes=VMEM_LIMIT_BYTES)
    state = pltpu.VMEM((RET_HEADS, RET_QK_DIM, RET_V_DIM), F32)
    qk_scratch = pltpu.VMEM((RET_HEADS, ts, RET_QK_DIM), F32)

    fwd, v = pl.pallas_call(
        functools.partial(_ret_fwd_kernel, ts=ts),
        name="ret_fwd",
        grid=(B, nt),
        in_specs=[
            smem,
            pl.BlockSpec((1, ts, D), lambda b, t: (b, t, 0)),
            _const_spec((1, D)),
            w_cols(0), w_cols(2),
            pl.BlockSpec((ts, half), lambda b, t: (t, 0)),
            pl.BlockSpec((ts, half), lambda b, t: (t, 0)),
        ],
        out_specs=[pl.BlockSpec((1, ts, RET_V), lambda b, t: (b, t, 0)),
                   pl.BlockSpec((1, ts, RET_V), lambda b, t: (b, t, 0))],
        out_shape=[jax.ShapeDtypeStruct((B, S, RET_V), F32),
                   jax.ShapeDtypeStruct((B, S, RET_V), BF16)],
        scratch_shapes=[pltpu.VMEM((ts, D), BF16), qk_scratch, qk_scratch, state],
        compiler_params=cparams,
    )(log_gamma, x, norm_w.reshape(1, D), w_in, w_in, cos, sin)

    rev = lambda b, t: (b, nt - 1 - t, 0)
    return pl.pallas_call(
        functools.partial(_ret_bwd_kernel, ts=ts),
        name="ret_bwd",
        grid=(B, nt),
        in_specs=[
            smem,
            pl.BlockSpec((1, ts, D), rev),
            _const_spec((1, D)),
            w_cols(1), w_cols(3),
            pl.BlockSpec((ts, half), lambda b, t: (nt - 1 - t, 0)),
            pl.BlockSpec((ts, half), lambda b, t: (nt - 1 - t, 0)),
            pl.BlockSpec((1, ts, RET_V), rev),
            pl.BlockSpec((1, ts, RET_V), rev),
            _const_spec((RET_V, D)),
        ],
        out_specs=pl.BlockSpec((1, ts, D), rev),
        out_shape=jax.ShapeDtypeStruct((B, S, D), F32),
        scratch_shapes=[pltpu.VMEM((ts, D), BF16), qk_scratch, qk_scratch,
                        pltpu.VMEM((ts, RET_V), BF16), state],
        compiler_params=cparams,
    )(log_gamma, x, norm_w.reshape(1, D), w_in, w_in, cos, sin, fwd, v, w_out.astype(BF16))


def kernel(x, pool_norm, pool_w_in, pool_w_group, pool_scale, pool_w_out,
           att_norm, att_w_in, att_q_norm, att_k_norm, att_w_out,
           ret_norm, ret_w_in, ret_decay, ret_w_out):
    depth = pool_norm.shape[0] + att_norm.shape[0] + ret_norm.shape[0]
    for layer in range(depth):
        kind, idx = layer % 3, layer // 3
        if kind == 0:
            x = _pool_layer(x, pool_norm[idx], pool_w_in[idx], pool_w_group[idx], pool_scale[idx],
                            pool_w_out[idx])
        elif kind == 1:
            x = _att_layer(x, att_norm[idx], att_w_in[idx], att_q_norm[idx], att_k_norm[idx],
                           att_w_out[idx])
        else:
            x = _ret_layer(x, ret_norm[idx], ret_w_in[idx], ret_decay[idx], ret_w_out[idx])
    return x
```
